```python
import math
import jax, jax.numpy as jnp
from jax import lax
import numpy as np

D_MODEL = 2048
BATCH = 4
SEQ = 8192
DEPTH = 1
DEC_BATCH = 8
DEC_SEQ = 2048
PAST_LEN = 128

HEAD_DIM = 128
N_HEADS_TOTAL = D_MODEL // HEAD_DIM
HA = N_HEADS_TOTAL // 2
KVA = max(1, HA // 4)
GA = HA // KVA
HB = N_HEADS_TOTAL - HA
WIDTH_A = HA * HEAD_DIM
WIDTH_B = HB * HEAD_DIM
MIX_WIDTH = WIDTH_A + WIDTH_B
PROJ_SPLITS = [WIDTH_A, KVA * HEAD_DIM, KVA * HEAD_DIM, WIDTH_B, WIDTH_B, WIDTH_B]
PROJ_WIDTH = sum(PROJ_SPLITS)

WINDOW = 128
BLOCK = 128
N_BUCKETS = 32
MAX_DISTANCE = 128

GRID_W = 64
NA_ROWS_MAX = 8
NA_COLS = 16
NA_ROW_BLOCK = 2
NA_COL_BLOCK = 16
NA_COL_SLAB = 32

N_EXPERTS = 16
EXPERT_FF = D_MODEL
CAPACITY_FACTOR = 2
EPS = 1e-6

kernel_name = "hybrid_window_gqa_natten_ec_encoder"


def rms_norm(x, g):
    xf = x.astype(jnp.float32)
    y = xf * lax.rsqrt(jnp.mean(xf * xf, axis=-1, keepdims=True) + EPS)
    return (y * g.astype(jnp.float32)).astype(x.dtype)


def t5_bucket(rel):
    nb = N_BUCKETS // 2
    max_exact = nb // 2
    ret = np.where(rel > 0, nb, 0)
    n = np.abs(rel)
    nf = np.maximum(n, 1).astype(np.float32)
    large = max_exact + (np.log(nf / max_exact) / math.log(MAX_DISTANCE / max_exact) * (nb - max_exact)).astype(np.int32)
    large = np.minimum(large, nb - 1)
    return (ret + np.where(n < max_exact, n, large)).astype(np.int32)


def window_gqa(q, k, v, t5_table, sink):
    B, S = q.shape[0], q.shape[1]
    nb = S // BLOCK
    qb = q.reshape(B, nb, BLOCK, KVA, GA, HEAD_DIM)
    pad = ((0, 0), (BLOCK, BLOCK), (0, 0), (0, 0))

    def band(t):
        tb = jnp.pad(t, pad).reshape(B, nb + 2, BLOCK, KVA, HEAD_DIM)
        return jnp.concatenate([tb[:, :-2], tb[:, 1:-1], tb[:, 2:]], axis=2)

    kb, vb = band(k), band(v)
    s = jnp.einsum('bnqkgd,bnjkd->bnkgqj', qb, kb, preferred_element_type=jnp.float32) * (HEAD_DIM ** -0.5)
    il = np.arange(BLOCK)[:, None]
    jl = np.arange(3 * BLOCK)[None, :]
    rel = jl - BLOCK - il
    bias = t5_table[t5_bucket(rel)].astype(jnp.float32)
    bias = jnp.transpose(bias, (2, 0, 1)).reshape(KVA, GA, BLOCK, 3 * BLOCK)
    kpos = np.arange(nb)[:, None, None] * BLOCK + jl[None] - BLOCK
    mask = (np.abs(rel) <= WINDOW)[None] & (kpos >= 0) & (kpos < S)
    s = jnp.where(jnp.asarray(mask)[None, :, None, None], s + bias, -jnp.inf)
    sk = sink.astype(jnp.float32).reshape(KVA, GA, 1, 1)
    m = jnp.maximum(jnp.max(s, axis=-1, keepdims=True), sk)
    p = jnp.exp(s - m)
    denom = jnp.sum(p, axis=-1, keepdims=True) + jnp.exp(sk - m)
    o = jnp.einsum('bnkgqj,bnjkd->bnqkgd', (p / denom).astype(v.dtype), vb)
    return o.reshape(B, S, WIDTH_A)


def neighbourhood_attn(q, k, v, rpb):
    B, S = q.shape[0], q.shape[1]
    rows = S // GRID_W
    kr = min(NA_ROWS_MAX, rows)
    sr = min(kr + NA_ROW_BLOCK - 1, rows)
    n_rb = rows // NA_ROW_BLOCK
    n_cb = GRID_W // NA_COL_BLOCK
    qg = q.reshape(B, rows, GRID_W, HB, HEAD_DIM)
    kg = k.reshape(B, rows, GRID_W, HB, HEAD_DIM)
    vg = v.reshape(B, rows, GRID_W, HB, HEAD_DIM)
    qcol = np.arange(GRID_W).reshape(n_cb, NA_COL_BLOCK)
    col_start = np.clip(qcol - NA_COLS // 2, 0, GRID_W - NA_COLS)
    slab_start = np.clip(np.arange(n_cb) * NA_COL_BLOCK - NA_COLS // 2, 0, GRID_W - NA_COL_SLAB)
    kcol = slab_start[:, None] + np.arange(NA_COL_SLAB)
    col_mask = (kcol[:, None, :] >= col_start[..., None]) & (kcol[:, None, :] < col_start[..., None] + NA_COLS)
    dc = np.clip(kcol[:, None, :] - qcol[..., None] + NA_COLS - 1, 0, 2 * NA_COLS - 2).astype(np.int32)
    col_mask = jnp.asarray(col_mask)
    kslab = kg[:, :, kcol]
    vslab = vg[:, :, kcol]
    scale = HEAD_DIM ** -0.5

    def row_block(p):
        r0 = p * NA_ROW_BLOCK
        qrow = r0 + jnp.arange(NA_ROW_BLOCK)
        row_start = jnp.clip(qrow - kr // 2, 0, rows - kr)
        ss = jnp.minimum(row_start[0], rows - sr)
        krow = ss + jnp.arange(sr)
        q_blk = lax.dynamic_slice_in_dim(qg, r0, NA_ROW_BLOCK, axis=1).reshape(B, NA_ROW_BLOCK, n_cb, NA_COL_BLOCK, HB, HEAD_DIM)
        k_blk = lax.dynamic_slice_in_dim(kslab, ss, sr, axis=1)
        v_blk = lax.dynamic_slice_in_dim(vslab, ss, sr, axis=1)
        s = jnp.einsum('brmqhd,bsmkhd->bhrmqsk', q_blk, k_blk, preferred_element_type=jnp.float32) * scale
        row_mask = (krow[None, :] >= row_start[:, None]) & (krow[None, :] < row_start[:, None] + kr)
        dr = jnp.clip(krow[None, :] - qrow[:, None] + NA_ROWS_MAX - 1, 0, 2 * NA_ROWS_MAX - 2)
        bias = rpb[:, dr[:, None, None, :, None], dc[None, :, :, None, :]].astype(jnp.float32)
        mask = row_mask[:, None, None, :, None] & col_mask[None, :, :, None, :]
        s = jnp.where(mask, s + bias, -jnp.inf)
        shp = s.shape
        pr = jax.nn.softmax(s.reshape(shp[:-2] + (shp[-2] * shp[-1],)), axis=-1).reshape(shp)
        o = jnp.einsum('bhrmqsk,bsmkhd->brmqhd', pr.astype(v.dtype), v_blk)
        return o.reshape(B, NA_ROW_BLOCK, GRID_W, WIDTH_B)

    out = lax.map(row_block, jnp.arange(n_rb))
    return jnp.moveaxis(out, 0, 1).reshape(B, S, WIDTH_B)


def hybrid_mixer(x, g_attn, w_in, qn_a, kn_a, qn_b, kn_b, sink_a, rpb_b, on_a, on_b, w_out, t5_table):
    B, S, _ = x.shape
    h = rms_norm(x, g_attn)
    proj = h @ w_in
    qa, ka, va, qb, kb, vb = jnp.split(proj, [int(c) for c in np.cumsum(PROJ_SPLITS)[:-1]], axis=-1)
    qa = rms_norm(qa.reshape(B, S, HA, HEAD_DIM), qn_a)
    ka = rms_norm(ka.reshape(B, S, KVA, HEAD_DIM), kn_a)
    va = va.reshape(B, S, KVA, HEAD_DIM)
    qb = rms_norm(qb.reshape(B, S, HB, HEAD_DIM), qn_b)
    kb = rms_norm(kb.reshape(B, S, HB, HEAD_DIM), kn_b)
    vb = vb.reshape(B, S, HB, HEAD_DIM)
    oa = rms_norm(window_gqa(qa, ka, va, t5_table, sink_a), on_a)
    ob = rms_norm(neighbourhood_attn(qb, kb, vb, rpb_b), on_b)
    return jnp.concatenate([oa, ob], axis=-1) @ w_out


def expert_choice_ffn(x, g_ffn, w_router, w_gate, w_up, w_down):
    B, S, D = x.shape
    n = B * S
    cap = CAPACITY_FACTOR * n // N_EXPERTS
    h = rms_norm(x, g_ffn).reshape(n, D)
    aff = jax.nn.softmax((h @ w_router).astype(jnp.float32), axis=-1)
    gate, idx = lax.top_k(aff.T, cap)
    xe = h[idx]
    hid = jax.nn.silu(jnp.einsum('ecd,edf->ecf', xe, w_gate)) * jnp.einsum('ecd,edf->ecf', xe, w_up)
    ye = jnp.einsum('ecf,efd->ecd', hid, w_down) * gate[..., None].astype(x.dtype)
    y = jnp.zeros((n, D), x.dtype).at[idx.reshape(-1)].add(ye.reshape(-1, D))
    return y.reshape(B, S, D)


def setup_inputs(seed: int = 0) -> dict:
    key = jax.random.key(seed)
    ks = jax.random.split(key, 20)
    f32 = jnp.float32

    def nrm(k, shape, scale):
        return jax.random.normal(k, shape, f32) * scale

    def gain(k, shape):
        return 1.0 + 0.02 * jax.random.normal(k, shape, f32)

    return {
        "x_prompt": nrm(ks[0], (BATCH, SEQ, D_MODEL), 1.0),
        "x_sample": nrm(ks[1], (DEC_BATCH, DEC_SEQ, D_MODEL), 1.0),
        "t5_table": nrm(ks[2], (N_BUCKETS, HA), 0.5),
        "norm_attn": gain(ks[3], (DEPTH, D_MODEL)),
        "w_in": nrm(ks[4], (DEPTH, D_MODEL, PROJ_WIDTH), D_MODEL ** -0.5),
        "q_norm_a": gain(ks[5], (DEPTH, HEAD_DIM)),
        "k_norm_a": gain(ks[6], (DEPTH, HEAD_DIM)),
        "q_norm_b": gain(ks[7], (DEPTH, HEAD_DIM)),
        "k_norm_b": gain(ks[8], (DEPTH, HEAD_DIM)),
        "sink_a": nrm(ks[9], (DEPTH, HA), 1.0),
        "rpb_b": nrm(ks[10], (DEPTH, HB, 2 * NA_ROWS_MAX - 1, 2 * NA_COLS - 1), 0.5),
        "out_norm_a": gain(ks[11], (DEPTH, WIDTH_A)),
        "out_norm_b": gain(ks[12], (DEPTH, WIDTH_B)),
        "w_out": nrm(ks[13], (DEPTH, MIX_WIDTH, D_MODEL), MIX_WIDTH ** -0.5),
        "norm_ffn": gain(ks[14], (DEPTH, D_MODEL)),
        "w_router": nrm(ks[15], (DEPTH, D_MODEL, N_EXPERTS), D_MODEL ** -0.5),
        "w_gate": nrm(ks[16], (DEPTH, N_EXPERTS, D_MODEL, EXPERT_FF), D_MODEL ** -0.5),
        "w_up": nrm(ks[17], (DEPTH, N_EXPERTS, D_MODEL, EXPERT_FF), D_MODEL ** -0.5),
        "w_down": nrm(ks[18], (DEPTH, N_EXPERTS, EXPERT_FF, D_MODEL), EXPERT_FF ** -0.5),
    }


def reference(x_prompt, x_sample, t5_table, norm_attn, w_in, q_norm_a, k_norm_a, q_norm_b, k_norm_b,
              sink_a, rpb_b, out_norm_a, out_norm_b, w_out, norm_ffn, w_router, w_gate, w_up, w_down):
    def trunk(x):
        for l in range(DEPTH):
            x = x + hybrid_mixer(x, norm_attn[l], w_in[l], q_norm_a[l], k_norm_a[l], q_norm_b[l], k_norm_b[l],
                                 sink_a[l], rpb_b[l], out_norm_a[l], out_norm_b[l], w_out[l], t5_table)
            x = x + expert_choice_ffn(x, norm_ffn[l], w_router[l], w_gate[l], w_up[l], w_down[l])
        return x

    y_prompt = trunk(x_prompt)
    y_sample = trunk(x_sample)
    return (y_prompt, y_sample)
```

```python
import functools

import numpy as np
import jax
import jax.numpy as jnp
from jax import lax
from jax.experimental import pallas as pl
from jax.experimental.pallas import tpu as pltpu

F32 = jnp.float32
BF16 = jnp.bfloat16
I32 = jnp.int32
U32 = jnp.uint32

D_MODEL = 2048
HEAD_DIM = 128
HA = 8
KVA = 2
GA = HA // KVA
HB = 8
WIDTH_A = HA * HEAD_DIM
WIDTH_B = HB * HEAD_DIM
PROJ_WIDTH = WIDTH_A + 2 * KVA * HEAD_DIM + 3 * WIDTH_B
WINDOW = 128
BLOCK = 128
N_BUCKETS = 32
MAX_DISTANCE = 128
GRID_W = 64
NA_ROWS = 8
NA_COLS = 16
NA_QROWS = 4
NA_KROWS = 3 * NA_QROWS
N_EXPERTS = 16
EXPERT_FF = D_MODEL
CAPACITY_FACTOR = 2
EPS = 1e-6
NEG = -1e30
LANES = 128
VMEM_LIMIT = 56 * 1024 * 1024


def _cparams(sem, vmem=VMEM_LIMIT):
    return pltpu.CompilerParams(dimension_semantics=sem, vmem_limit_bytes=vmem)


def _dot_nt(a, b):
    return lax.dot_general(a, b, (((1,), (1,)), ((), ())), preferred_element_type=F32)


def _in_proj_kernel(x_ref, g_ref, w_ref, gain_ref, flag_ref, o_ref, xn_ref):
    @pl.when(pl.program_id(1) == 0)
    def _():
        x = x_ref[...]
        ms = jnp.mean(x * x, axis=-1, keepdims=True)
        xn_ref[...] = (x * lax.rsqrt(ms + EPS) * g_ref[...]).astype(BF16)

    acc = jnp.dot(xn_ref[...], w_ref[...], preferred_element_type=F32)
    for h in range(acc.shape[1] // HEAD_DIM):
        sl = slice(h * HEAD_DIM, (h + 1) * HEAD_DIM)
        a = acc[:, sl]
        r = lax.rsqrt(jnp.mean(a * a, axis=-1, keepdims=True) + EPS)
        scale = jnp.where(flag_ref[:, sl] > 0.0, r, 1.0) * gain_ref[:, sl]
        o_ref[:, sl] = (a * scale).astype(BF16)


def _in_proj(x2, g_attn, w_perm, gain, flag, tm, tn):
    n = x2.shape[0]
    return pl.pallas_call(
        _in_proj_kernel,
        out_shape=jax.ShapeDtypeStruct((n, PROJ_WIDTH), BF16),
        grid=(n // tm, PROJ_WIDTH // tn),
        in_specs=[
            pl.BlockSpec((tm, D_MODEL), lambda i, j: (i, 0)),
            pl.BlockSpec((1, D_MODEL), lambda i, j: (0, 0)),
            pl.BlockSpec((D_MODEL, tn), lambda i, j: (0, j)),
            pl.BlockSpec((1, tn), lambda i, j: (0, j)),
            pl.BlockSpec((1, tn), lambda i, j: (0, j)),
        ],
        out_specs=pl.BlockSpec((tm, tn), lambda i, j: (i, j)),
        scratch_shapes=[pltpu.VMEM((tm, D_MODEL), BF16)],
        compiler_params=_cparams(("parallel", "arbitrary")),
        name="in_proj",
    )(x2, g_attn, w_perm, gain, flag)


def _win_kernel(q_ref, kp_ref, kc_ref, kn_ref, vp_ref, vc_ref, vn_ref, bias_ref, sink_ref, g_ref,
                o_ref, obuf_ref):
    i = pl.program_id(1)
    seq = pl.num_programs(1) * BLOCK
    kpos = i * BLOCK - BLOCK + lax.broadcasted_iota(I32, (1, 3 * BLOCK), 1)
    pen = jnp.where((kpos >= 0) & (kpos < seq), 0.0, NEG).astype(F32)
    for kh in range(KVA):
        ks = slice(kh * HEAD_DIM, (kh + 1) * HEAD_DIM)
        qs = jnp.concatenate(
            [q_ref[0, :, (kh * GA + g) * HEAD_DIM:(kh * GA + g + 1) * HEAD_DIM] for g in range(GA)], axis=0)
        kw = jnp.concatenate([kp_ref[0, :, ks], kc_ref[0, :, ks], kn_ref[0, :, ks]], axis=0)
        vw = jnp.concatenate([vp_ref[0, :, ks], vc_ref[0, :, ks], vn_ref[0, :, ks]], axis=0)
        s = _dot_nt(qs, kw) + bias_ref[kh] + pen
        sk = sink_ref[kh]
        m = jnp.maximum(jnp.max(s, axis=-1, keepdims=True), sk)
        p = jnp.exp(s - m)
        denom = jnp.sum(p, axis=-1, keepdims=True) + jnp.exp(sk - m)
        o = jnp.dot(p.astype(BF16), vw, preferred_element_type=F32) / denom
        for g in range(GA):
            hs = slice((kh * GA + g) * HEAD_DIM, (kh * GA + g + 1) * HEAD_DIM)
            obuf_ref[:, hs] = o[g * BLOCK:(g + 1) * BLOCK, :]
    oall = obuf_ref[...]
    r = lax.rsqrt(jnp.mean(oall * oall, axis=-1, keepdims=True) + EPS)
    o_ref[0] = (oall * r * g_ref[...]).astype(BF16)


def _win_attn(proj3, bias_st, sink_st, on_a):
    b, s, _ = proj3.shape
    nb = s // BLOCK
    kcol = (WIDTH_A + 3 * WIDTH_B) // (KVA * HEAD_DIM)
    kw = KVA * HEAD_DIM

    def kspec(col, off):
        return pl.BlockSpec((1, BLOCK, kw), lambda bi, i: (bi, jnp.clip(i + off, 0, nb - 1), col))

    return pl.pallas_call(
        _win_kernel,
        out_shape=jax.ShapeDtypeStruct((b, s, WIDTH_A), BF16),
        grid=(b, nb),
        in_specs=[
            pl.BlockSpec((1, BLOCK, WIDTH_A), lambda bi, i: (bi, i, 0)),
            kspec(kcol, -1), kspec(kcol, 0), kspec(kcol, 1),
            kspec(kcol + 1, -1), kspec(kcol + 1, 0), kspec(kcol + 1, 1),
            pl.BlockSpec((KVA, GA * BLOCK, 3 * BLOCK), lambda bi, i: (0, 0, 0)),
            pl.BlockSpec((KVA, GA * BLOCK, 1), lambda bi, i: (0, 0, 0)),
            pl.BlockSpec((1, WIDTH_A), lambda bi, i: (0, 0)),
        ],
        out_specs=pl.BlockSpec((1, BLOCK, WIDTH_A), lambda bi, i: (bi, i, 0)),
        scratch_shapes=[pltpu.VMEM((BLOCK, WIDTH_A), F32)],
        compiler_params=_cparams(("parallel", "arbitrary")),
        name="win_attn",
    )(proj3, proj3, proj3, proj3, proj3, proj3, proj3, bias_st, sink_st, on_a)


def _na_kernel(q_ref, k0_ref, k1_ref, k2_ref, v0_ref, v1_ref, v2_ref, bias_ref, g_ref, o_ref, obuf_ref):
    tq = q_ref.shape[1]
    for h in range(HB):
        hs = slice(h * HEAD_DIM, (h + 1) * HEAD_DIM)
        qh = q_ref[0, :, hs]
        s = jnp.concatenate([_dot_nt(qh, kr[0, :, hs]) for kr in (k0_ref, k1_ref, k2_ref)], axis=1)
        s = s + bias_ref[0, h]
        m = jnp.max(s, axis=-1, keepdims=True)
        p = jnp.exp(s - m)
        denom = jnp.sum(p, axis=-1, keepdims=True)
        pb = p.astype(BF16)
        pv = jnp.dot(pb[:, 0:tq], v0_ref[0, :, hs], preferred_element_type=F32)
        pv = pv + jnp.dot(pb[:, tq:2 * tq], v1_ref[0, :, hs], preferred_element_type=F32)
        pv = pv + jnp.dot(pb[:, 2 * tq:3 * tq], v2_ref[0, :, hs], preferred_element_type=F32)
        obuf_ref[:, hs] = pv / denom
    oall = obuf_ref[...]
    r = lax.rsqrt(jnp.mean(oall * oall, axis=-1, keepdims=True) + EPS)
    o_ref[0] = (oall * r * g_ref[...]).astype(BF16)


def _na_attn(proj3, bias_pat, on_b):
    b, s, _ = proj3.shape
    tq = NA_QROWS * GRID_W
    nblk = s // tq

    def kvspec(col, off):
        return pl.BlockSpec((1, tq, WIDTH_B), lambda bi, i: (bi, jnp.clip(i - 1, 0, nblk - 3) + off, col))

    def pat(bi, i):
        return (jnp.where(i == 0, 0, jnp.where(i == nblk - 1, 2, 1)), 0, 0, 0)

    return pl.pallas_call(
        _na_kernel,
        out_shape=jax.ShapeDtypeStruct((b, s, WIDTH_B), BF16),
        grid=(b, nblk),
        in_specs=[
            pl.BlockSpec((1, tq, WIDTH_B), lambda bi, i: (bi, i, 1)),
            kvspec(2, 0), kvspec(2, 1), kvspec(2, 2),
            kvspec(3, 0), kvspec(3, 1), kvspec(3, 2),
            pl.BlockSpec((1, HB, tq, 3 * tq), pat),
            pl.BlockSpec((1, WIDTH_B), lambda bi, i: (0, 0)),
        ],
        out_specs=pl.BlockSpec((1, tq, WIDTH_B), lambda bi, i: (bi, i, 0)),
        scratch_shapes=[pltpu.VMEM((tq, WIDTH_B), F32)],
        compiler_params=_cparams(("parallel", "arbitrary")),
        name="na_attn",
    )(proj3, proj3, proj3, proj3, proj3, proj3, proj3, bias_pat, on_b)


def _out_proj_kernel(oa_ref, ob_ref, wo_ref, x_ref, g_ref, wr_ref, x1_ref, hp_ref, lg_ref):
    tm = x_ref.shape[0]
    half = D_MODEL // 2
    acc = jnp.dot(oa_ref[...], wo_ref[0:WIDTH_A, :], preferred_element_type=F32)
    acc = acc + jnp.dot(ob_ref[...], wo_ref[WIDTH_A:WIDTH_A + WIDTH_B, :], preferred_element_type=F32)
    x1 = x_ref[...] + acc
    x1_ref[...] = x1
    h32 = x1 * lax.rsqrt(jnp.mean(x1 * x1, axis=-1, keepdims=True) + EPS) * g_ref[...]
    h_hi = h32.astype(BF16)
    h_hi32 = h_hi.astype(F32)
    bits = pltpu.bitcast(h_hi32, U32)
    hp_ref[...] = (bits[:, 0:half] >> 16) | (bits[:, half:D_MODEL] & jnp.uint32(0xFFFF0000))
    h_lo = (h32 - h_hi32).astype(BF16)
    r = jnp.dot(jnp.concatenate([h_hi, h_lo], axis=0), wr_ref[...], preferred_element_type=F32)
    ne = N_EXPERTS
    lg_ref[...] = (r[0:tm, 0:ne] + r[0:tm, ne:2 * ne]) + (r[tm:2 * tm, 0:ne] + r[tm:2 * tm, ne:2 * ne])


def _out_proj(oa, ob, wo, x2, g_ffn, wr2, tm):
    n = x2.shape[0]
    return pl.pallas_call(
        _out_proj_kernel,
        out_shape=(jax.ShapeDtypeStruct((n, D_MODEL), F32),
                   jax.ShapeDtypeStruct((n, D_MODEL // 2), U32),
                   jax.ShapeDtypeStruct((n, N_EXPERTS), F32)),
        grid=(n // tm,),
        in_specs=[
            pl.BlockSpec((tm, WIDTH_A), lambda i: (i, 0)),
            pl.BlockSpec((tm, WIDTH_B), lambda i: (i, 0)),
            pl.BlockSpec((WIDTH_A + WIDTH_B, D_MODEL), lambda i: (0, 0)),
            pl.BlockSpec((tm, D_MODEL), lambda i: (i, 0)),
            pl.BlockSpec((1, D_MODEL), lambda i: (0, 0)),
            pl.BlockSpec((D_MODEL, 2 * N_EXPERTS), lambda i: (0, 0)),
        ],
        out_specs=(pl.BlockSpec((tm, D_MODEL), lambda i: (i, 0)),
                   pl.BlockSpec((tm, D_MODEL // 2), lambda i: (i, 0)),
                   pl.BlockSpec((tm, N_EXPERTS), lambda i: (i, 0))),
        compiler_params=_cparams(("parallel",)),
        name="out_proj",
    )(oa, ob, wo, x2, g_ffn, wr2)


def _route_kernel(lg_ref, idx_ref, gate_ref, aff_ref, *, cap):
    ne, rr, _ = lg_ref.shape
    lg = lg_ref[...]
    ex = jnp.exp(lg - jnp.max(lg, axis=0, keepdims=True))
    aff_ref[...] = ex / jnp.sum(ex, axis=0, keepdims=True)

    upper = (lax.broadcasted_iota(I32, (LANES, LANES), 0) <= lax.broadcasted_iota(I32, (LANES, LANES), 1)).astype(BF16)
    lower = (lax.broadcasted_iota(I32, (rr, rr), 1) < lax.broadcasted_iota(I32, (rr, rr), 0)).astype(BF16)
    pcol = lax.broadcasted_iota(I32, (1, cap), 1).astype(F32)
    rowid = lax.broadcasted_iota(I32, (rr, 1), 0).astype(F32)
    laneid = lax.broadcasted_iota(I32, (LANES, 1), 0).astype(F32)

    def total(v):
        return jnp.sum(jnp.sum(v, axis=1, keepdims=True), axis=0, keepdims=True)

    def prefix(mask):
        within = jnp.dot(mask.astype(BF16), upper, preferred_element_type=F32)
        tot = within[:, LANES - 1:LANES]
        off = jnp.dot(lower, jnp.broadcast_to(tot, (rr, LANES)).astype(BF16), preferred_element_type=F32)
        return within + off, off[:, 0:1], tot

    def gather_rows(vals_t, onehot_t):
        return jnp.dot(vals_t.astype(BF16), onehot_t, preferred_element_type=F32)

    def body(e, carry):
        a = aff_ref[e]
        bits = pltpu.bitcast(a, I32)
        thr = jnp.zeros((1, 1), I32)
        for bit in range(30, -1, -1):
            cand = thr | jnp.int32(1 << bit)
            cnt = total((bits >= cand).astype(F32))
            thr = jnp.where(cnt >= cap, cand, thr)
        gt = bits > thr
        eq = bits == thr
        need = cap - total(gt.astype(F32))
        eq_rank, _, _ = prefix(eq.astype(F32))
        sel = gt | (eq & (eq_rank <= need))
        pos, excl, tot = prefix(sel.astype(F32))
        onehot_t = ((excl <= pcol) & (pcol < excl + tot))
        oh = onehot_t.astype(BF16)
        row_of = jnp.sum(jnp.where(onehot_t, rowid, 0.0), axis=0, keepdims=True)
        pos_t = pos.T
        pos_hi = jnp.floor(pos_t * (1.0 / 64.0))
        pos_lo = pos_t - 64.0 * pos_hi
        prow = 64.0 * gather_rows(pos_hi, oh) + gather_rows(pos_lo, oh)
        local = jnp.sum((prow <= pcol).astype(F32), axis=0, keepdims=True)
        idx_ref[e] = (row_of * float(LANES) + local).astype(I32)
        a_t = a.T
        a1 = a_t.astype(BF16).astype(F32)
        a2 = (a_t - a1).astype(BF16).astype(F32)
        a3 = a_t - a1 - a2
        arow = gather_rows(a1, oh) + gather_rows(a2, oh) + gather_rows(a3, oh)
        gate_ref[e] = jnp.sum(jnp.where(laneid == local, arow, 0.0), axis=0, keepdims=True)
        return carry

    lax.fori_loop(0, ne, body, 0)


def _route(lg3, cap):
    ne, rr, _ = lg3.shape
    return pl.pallas_call(
        functools.partial(_route_kernel, cap=cap),
        out_shape=(jax.ShapeDtypeStruct((ne, 1, cap), I32), jax.ShapeDtypeStruct((ne, 1, cap), F32)),
        scratch_shapes=[pltpu.VMEM((ne, rr, LANES), F32)],
        compiler_params=pltpu.CompilerParams(vmem_limit_bytes=VMEM_LIMIT),
        name="route",
    )(lg3)


def _ffn_kernel(idx_ref, gate_ref, hp_hbm, y_in_hbm, wg_ref, wu_ref, wd_ref, y_hbm,
                gbuf, xbuf, acc, ybuf, sem_h, sem_yi, sem_yo):
    del y_in_hbm
    tm = gbuf.shape[0]
    half = D_MODEL // 2
    e = pl.program_id(0)
    c = pl.program_id(2)
    nc = pl.num_programs(2)

    def h_copy(r):
        t = idx_ref[0, 0, r]
        return pltpu.make_async_copy(hp_hbm.at[pl.ds(t, 1), :], gbuf.at[pl.ds(r, 1), :], sem_h)

    def yi_copy(r):
        t = idx_ref[0, 0, r]
        return pltpu.make_async_copy(y_hbm.at[pl.ds(t, 1), :], ybuf.at[pl.ds(r, 1), :], sem_yi)

    def yo_copy(r):
        t = idx_ref[0, 0, r]
        return pltpu.make_async_copy(ybuf.at[pl.ds(r, 1), :], y_hbm.at[pl.ds(t, 1), :], sem_yo)

    def for_rows(fn):
        def body(r, carry):
            fn(r)
            return carry
        lax.fori_loop(0, tm, body, 0, unroll=8)

    @pl.when(c == 0)
    def _():
        for_rows(lambda r: h_copy(r).start())
        for_rows(lambda r: yi_copy(r).start())
        for_rows(lambda r: h_copy(r).wait())
        w = gbuf[...]
        xbuf[:, 0:half] = pltpu.bitcast(w << 16, F32).astype(BF16)
        xbuf[:, half:D_MODEL] = pltpu.bitcast(w & jnp.uint32(0xFFFF0000), F32).astype(BF16)
        acc[...] = jnp.zeros_like(acc)

    x = xbuf[...]
    g = jnp.dot(x, wg_ref[0], preferred_element_type=F32)
    u = jnp.dot(x, wu_ref[0], preferred_element_type=F32)
    hid = (g * jax.nn.sigmoid(g)) * u
    acc[...] += jnp.dot(hid.astype(BF16), wd_ref[0], preferred_element_type=F32)

    @pl.when(c == nc - 1)
    def _():
        for_rows(lambda r: yi_copy(r).wait())
        onehot = lax.broadcasted_iota(I32, (1, N_EXPERTS), 1) == e
        gcol = jnp.sum(jnp.where(onehot, gate_ref[...], 0.0), axis=1, keepdims=True)
        ybuf[...] = ybuf[...] + acc[...] * gcol
        for_rows(lambda r: yo_copy(r).start())
        for_rows(lambda r: yo_copy(r).wait())


def _ffn(idx3, gate_t, hp, x1, wg, wu, wd, tm, tf):
    ne = N_EXPERTS
    cap = gate_t.shape[0]
    nj = cap // tm
    nc = EXPERT_FF // tf
    n = x1.shape[0]
    return pl.pallas_call(
        _ffn_kernel,
        out_shape=jax.ShapeDtypeStruct((n, D_MODEL), F32),
        grid=(ne, nj, nc),
        in_specs=[
            pl.BlockSpec((1, 1, tm), lambda e, j, c: (e * nj + j, 0, 0), memory_space=pltpu.SMEM),
            pl.BlockSpec((tm, ne), lambda e, j, c: (j, 0)),
            pl.BlockSpec(memory_space=pl.ANY),
            pl.BlockSpec(memory_space=pl.ANY),
            pl.BlockSpec((1, D_MODEL, tf), lambda e, j, c: (e, 0, c)),
            pl.BlockSpec((1, D_MODEL, tf), lambda e, j, c: (e, 0, c)),
            pl.BlockSpec((1, tf, D_MODEL), lambda e, j, c: (e, c, 0)),
        ],
        out_specs=pl.BlockSpec(memory_space=pl.ANY),
        scratch_shapes=[
            pltpu.VMEM((tm, D_MODEL // 2), U32),
            pltpu.VMEM((tm, D_MODEL), BF16),
            pltpu.VMEM((tm, D_MODEL), F32),
            pltpu.VMEM((tm, D_MODEL), F32),
            pltpu.SemaphoreType.DMA(()),
            pltpu.SemaphoreType.DMA(()),
            pltpu.SemaphoreType.DMA(()),
        ],
        input_output_aliases={3: 0},
        compiler_params=_cparams(("arbitrary", "arbitrary", "arbitrary")),
        name="ffn",
    )(idx3, gate_t, hp, x1, wg, wu, wd)


def _t5_bucket(rel):
    nb = N_BUCKETS // 2
    max_exact = nb // 2
    ret = np.where(rel > 0, nb, 0)
    n = np.abs(rel)
    nf = np.maximum(n, 1).astype(np.float32)
    large = max_exact + (np.log(nf / max_exact) / np.log(MAX_DISTANCE / max_exact) * (nb - max_exact)).astype(np.int32)
    large = np.minimum(large, nb - 1)
    return (ret + np.where(n < max_exact, n, large)).astype(np.int32)


def _win_tables():
    il = np.arange(BLOCK)[:, None]
    jl = np.arange(3 * BLOCK)[None, :]
    rel = jl - BLOCK - il
    return _t5_bucket(rel), np.abs(rel) <= WINDOW


def _na_tables(rows):
    nblk = rows // NA_QROWS
    idx, valid = [], []
    for blk in (0, 1, nblk - 1):
        r0 = blk * NA_QROWS
        ws = int(np.clip(r0 - NA_ROWS // 2, 0, rows - NA_KROWS))
        qrow = (r0 + np.arange(NA_QROWS))[:, None, None, None]
        qcol = np.arange(GRID_W)[None, :, None, None]
        krow = (ws + np.arange(NA_KROWS))[None, None, :, None]
        kcol = np.arange(GRID_W)[None, None, None, :]
        rs = np.clip(qrow - NA_ROWS // 2, 0, rows - NA_ROWS)
        cs = np.clip(qcol - NA_COLS // 2, 0, GRID_W - NA_COLS)
        ok = (krow >= rs) & (krow < rs + NA_ROWS) & (kcol >= cs) & (kcol < cs + NA_COLS)
        dr = np.clip(krow - qrow + NA_ROWS - 1, 0, 2 * NA_ROWS - 2)
        dc = np.clip(kcol - qcol + NA_COLS - 1, 0, 2 * NA_COLS - 2)
        flat = np.broadcast_to(dr * (2 * NA_COLS - 1) + dc, ok.shape)
        tq = NA_QROWS * GRID_W
        idx.append(flat.reshape(tq, NA_KROWS * GRID_W))
        valid.append(np.broadcast_to(ok, flat.shape).reshape(tq, NA_KROWS * GRID_W))
    return np.stack(idx).astype(np.int32), np.stack(valid)


def _prepare(t5_table, norm_attn, w_in, q_norm_a, k_norm_a, q_norm_b, k_norm_b, sink_a, rpb_b,
             out_norm_a, out_norm_b, w_out, norm_ffn, w_router, w_gate, w_up, w_down):
    qa, ka, va = WIDTH_A, KVA * HEAD_DIM, KVA * HEAD_DIM
    o_ka, o_va, o_qb = qa, qa + ka, qa + ka + va
    o_kb, o_vb = o_qb + WIDTH_B, o_qb + 2 * WIDTH_B
    w_perm = jnp.concatenate([w_in[:, 0:qa], w_in[:, o_qb:o_kb], w_in[:, o_kb:o_vb], w_in[:, o_vb:],
                              w_in[:, o_ka:o_va], w_in[:, o_va:o_qb]], axis=1).astype(BF16)
    scale = HEAD_DIM ** -0.5
    ones = lambda k: jnp.ones((k,), F32)
    zeros = lambda k: jnp.zeros((k,), F32)
    gain = jnp.concatenate([jnp.tile(q_norm_a, HA) * scale, jnp.tile(q_norm_b, HB) * scale, jnp.tile(k_norm_b, HB),
                            ones(WIDTH_B), jnp.tile(k_norm_a, KVA), ones(va)])[None, :]
    flag = jnp.concatenate([ones(qa), ones(WIDTH_B), ones(WIDTH_B), zeros(WIDTH_B), ones(ka), zeros(va)])[None, :]

    bucket, band = _win_tables()
    bias = jnp.where(band[None], jnp.transpose(t5_table[bucket].astype(F32), (2, 0, 1)), NEG)
    bias_st = bias.reshape(KVA, GA * BLOCK, 3 * BLOCK)
    sink_st = jnp.repeat(sink_a.astype(F32), BLOCK).reshape(KVA, GA * BLOCK, 1)

    wr_hi = w_router.astype(BF16)
    wr_lo = (w_router - wr_hi.astype(F32)).astype(BF16)
    return dict(
        w_perm=w_perm, gain=gain, flag=flag, g_attn=norm_attn[None, :], bias_st=bias_st, sink_st=sink_st,
        rpb_flat=rpb_b.astype(F32).reshape(HB, -1), on_a=out_norm_a[None, :], on_b=out_norm_b[None, :],
        wo=w_out.astype(BF16), g_ffn=norm_ffn[None, :], wr2=jnp.concatenate([wr_hi, wr_lo], axis=1),
        wg=w_gate.astype(BF16), wu=w_up.astype(BF16), wd=w_down.astype(BF16))


def _pick(n, pref):
    t = pref
    while n % t:
        t //= 2
    return t


def _trunk(x, p):
    b, s, _ = x.shape
    n = b * s
    x2 = x.reshape(n, D_MODEL)
    proj = _in_proj(x2, p["g_attn"], p["w_perm"], p["gain"], p["flag"], _pick(n, 1024), 512)
    proj3 = proj.reshape(b, s, PROJ_WIDTH)
    oa = _win_attn(proj3, p["bias_st"], p["sink_st"], p["on_a"])
    na_idx, na_ok = _na_tables(s // GRID_W)
    bias_pat = jnp.where(na_ok[None], p["rpb_flat"][:, na_idx], NEG)
    ob = _na_attn(proj3, jnp.transpose(bias_pat, (1, 0, 2, 3)), p["on_b"])
    x1, hp, logits = _out_proj(oa.reshape(n, WIDTH_A), ob.reshape(n, WIDTH_B), p["wo"], x2, p["g_ffn"], p["wr2"],
                               _pick(n, 256))
    cap =CAPACITY_FACTOR * n // N_EXPERTS
    idx, gate = _route(logits.T.reshape(N_EXPERTS, n // LANES, LANES), cap)
    tm = _pick(cap, 512)
    y = _ffn(idx.reshape(N_EXPERTS * (cap // tm), 1, tm), gate.reshape(N_EXPERTS, cap).T, hp, x1,
             p["wg"], p["wu"], p["wd"], tm, 512)
    return y.reshape(b, s, D_MODEL)


def kernel(x_prompt, x_sample, t5_table, norm_attn, w_in, q_norm_a, k_norm_a, q_norm_b, k_norm_b, sink_a, rpb_b,
           out_norm_a, out_norm_b, w_out, norm_ffn, w_router, w_gate, w_up, w_down):
    p = _prepare(t5_table, norm_attn[0], w_in[0], q_norm_a[0], k_norm_a[0], q_norm_b[0], k_norm_b[0], sink_a[0],
                 rpb_b[0], out_norm_a[0], out_norm_b[0], w_out[0], norm_ffn[0], w_router[0], w_gate[0], w_up[0],
                 w_down[0])
    return (_trunk(x_prompt, p), _trunk(x_sample, p))
```

```python
import functools

import numpy as np
import jax
import jax.numpy as jnp
from jax import lax
from jax.experimental import pallas as pl
from jax.experimental.pallas import tpu as pltpu

F32 = jnp.float32
BF16 = jnp.bfloat16
I32 = jnp.int32
U32 = jnp.uint32

D_MODEL = 2048
HEAD_DIM = 128
HA = 8
KVA = 2
GA = HA // KVA
HB = 8
WIDTH_A = HA * HEAD_DIM
WIDTH_B = HB * HEAD_DIM
PROJ_WIDTH = WIDTH_A + 2 * KVA * HEAD_DIM + 3 * WIDTH_B
WINDOW = 128
BLOCK = 128
N_BUCKETS = 32
MAX_DISTANCE = 128
GRID_W = 64
NA_ROWS = 8
NA_COLS = 16
NA_QROWS = 4
NA_KROWS = 3 * NA_QROWS
N_EXPERTS = 16
EXPERT_FF = D_MODEL
CAPACITY_FACTOR = 2
EPS = 1e-6
NEG = -1e30
LANES = 128
VMEM_LIMIT = 56 * 1024 * 1024


def _cparams(sem, vmem=VMEM_LIMIT):
    return pltpu.CompilerParams(dimension_semantics=sem, vmem_limit_bytes=vmem)


def _dot_nt(a, b):
    return lax.dot_general(a, b, (((1,), (1,)), ((), ())), preferred_element_type=F32)


def _in_proj_kernel(x_ref, g_ref, w_ref, gain_ref, flag_ref, o_ref, xn_ref):
    @pl.when(pl.program_id(1) == 0)
    def _():
        x = x_ref[...]
        ms = jnp.mean(x * x, axis=-1, keepdims=True)
        xn_ref[...] = (x * lax.rsqrt(ms + EPS) * g_ref[...]).astype(BF16)

    acc = jnp.dot(xn_ref[...], w_ref[...], preferred_element_type=F32)
    for h in range(acc.shape[1] // HEAD_DIM):
        sl = slice(h * HEAD_DIM, (h + 1) * HEAD_DIM)
        a = acc[:, sl]
        r = lax.rsqrt(jnp.mean(a * a, axis=-1, keepdims=True) + EPS)
        scale = jnp.where(flag_ref[:, sl] > 0.0, r, 1.0) * gain_ref[:, sl]
        o_ref[:, sl] = (a * scale).astype(BF16)


def _in_proj(x2, g_attn, w_perm, gain, flag, tm, tn):
    n = x2.shape[0]
    return pl.pallas_call(
        _in_proj_kernel,
        out_shape=jax.ShapeDtypeStruct((n, PROJ_WIDTH), BF16),
        grid=(n // tm, PROJ_WIDTH // tn),
        in_specs=[
            pl.BlockSpec((tm, D_MODEL), lambda i, j: (i, 0)),
            pl.BlockSpec((1, D_MODEL), lambda i, j: (0, 0)),
            pl.BlockSpec((D_MODEL, tn), lambda i, j: (0, j)),
            pl.BlockSpec((1, tn), lambda i, j: (0, j)),
            pl.BlockSpec((1, tn), lambda i, j: (0, j)),
        ],
        out_specs=pl.BlockSpec((tm, tn), lambda i, j: (i, j)),
        scratch_shapes=[pltpu.VMEM((tm, D_MODEL), BF16)],
        compiler_params=_cparams(("parallel", "arbitrary")),
        name="in_proj",
    )(x2, g_attn, w_perm, gain, flag)


def _win_kernel(q_ref, kp_ref, kc_ref, kn_ref, vp_ref, vc_ref, vn_ref, bias_ref, sink_ref, g_ref,
                o_ref, obuf_ref):
    nsub = q_ref.shape[1] // BLOCK
    i = pl.program_id(1)
    seq = pl.num_programs(1) * nsub * BLOCK
    lane_pos = lax.broadcasted_iota(I32, (1, 3 * BLOCK), 1) - BLOCK

    def window(prev_ref, cur_ref, next_ref, sb, ks):
        lo, hi = (sb - 1) * BLOCK, (sb + 2) * BLOCK
        parts = []
        if lo < 0:
            parts.append(prev_ref[0, :, ks])
        parts.append(cur_ref[0, max(lo, 0):min(hi, nsub * BLOCK), ks])
        if hi > nsub * BLOCK:
            parts.append(next_ref[0, :, ks])
        return jnp.concatenate(parts, axis=0)

    for sb in range(nsub):
        rows = slice(sb * BLOCK, (sb + 1) * BLOCK)
        kpos = (i * nsub + sb) * BLOCK + lane_pos
        pen = jnp.where((kpos >= 0) & (kpos < seq), 0.0, NEG).astype(F32)
        for kh in range(KVA):
            ks = slice(kh * HEAD_DIM, (kh + 1) * HEAD_DIM)
            qs = jnp.concatenate(
                [q_ref[0, rows, (kh * GA + g) * HEAD_DIM:(kh * GA + g + 1) * HEAD_DIM] for g in range(GA)], axis=0)
            kw = window(kp_ref, kc_ref, kn_ref, sb, ks)
            vw = window(vp_ref, vc_ref, vn_ref, sb, ks)
            s = _dot_nt(qs, kw) + bias_ref[kh] + pen
            sk = sink_ref[kh]
            m = jnp.maximum(jnp.max(s, axis=-1, keepdims=True), sk)
            p = jnp.exp(s - m)
            denom = jnp.sum(p, axis=-1, keepdims=True) + jnp.exp(sk - m)
            o = jnp.dot(p.astype(BF16), vw, preferred_element_type=F32) / denom
            for g in range(GA):
                hs = slice((kh * GA + g) * HEAD_DIM, (kh * GA + g + 1) * HEAD_DIM)
                obuf_ref[rows, hs] = o[g * BLOCK:(g + 1) * BLOCK, :]
    oall = obuf_ref[...]
    r = lax.rsqrt(jnp.mean(oall * oall, axis=-1, keepdims=True) + EPS)
    o_ref[0] = (oall * r * g_ref[...]).astype(BF16)


def _win_attn(proj3, bias_st, sink_st, on_a, nsub):
    b, s, _ = proj3.shape
    tq = nsub * BLOCK
    nb = s // BLOCK
    kcol = (WIDTH_A + 3 * WIDTH_B) // (KVA * HEAD_DIM)
    kw = KVA * HEAD_DIM

    def edge(col, side):
        if side < 0:
            return pl.BlockSpec((1, BLOCK, kw), lambda bi, i: (bi, jnp.maximum(i * nsub - 1, 0), col))
        return pl.BlockSpec((1, BLOCK, kw), lambda bi, i: (bi, jnp.minimum(i * nsub + nsub, nb - 1), col))

    def main(col):
        return pl.BlockSpec((1, tq, kw), lambda bi, i: (bi, i, col))

    return pl.pallas_call(
        _win_kernel,
        out_shape=jax.ShapeDtypeStruct((b, s, WIDTH_A), BF16),
        grid=(b, s // tq),
        in_specs=[
            pl.BlockSpec((1, tq, WIDTH_A), lambda bi, i: (bi, i, 0)),
            edge(kcol, -1), main(kcol), edge(kcol, 1),
            edge(kcol + 1, -1), main(kcol + 1), edge(kcol + 1, 1),
            pl.BlockSpec((KVA, GA * BLOCK, 3 * BLOCK), lambda bi, i: (0, 0, 0)),
            pl.BlockSpec((KVA, GA * BLOCK, 1), lambda bi, i: (0, 0, 0)),
            pl.BlockSpec((1, WIDTH_A), lambda bi, i: (0, 0)),
        ],
        out_specs=pl.BlockSpec((1, tq, WIDTH_A), lambda bi, i: (bi, i, 0)),
        scratch_shapes=[pltpu.VMEM((tq, WIDTH_A), F32)],
        compiler_params=_cparams(("parallel", "arbitrary")),
        name="win_attn",
    )(proj3, proj3, proj3, proj3, proj3, proj3, proj3, bias_st, sink_st, on_a)


def _na_kernel(q_ref, k0_ref, k1_ref, k2_ref, v0_ref, v1_ref, v2_ref, bias_ref, g_ref, o_ref, obuf_ref):
    tq = q_ref.shape[1]
    for h in range(HB):
        hs = slice(h * HEAD_DIM, (h + 1) * HEAD_DIM)
        qh = q_ref[0, :, hs]
        s = jnp.concatenate([_dot_nt(qh, kr[0, :, hs]) for kr in (k0_ref, k1_ref, k2_ref)], axis=1)
        s = s + bias_ref[0, h]
        m = jnp.max(s, axis=-1, keepdims=True)
        p = jnp.exp(s - m)
        denom = jnp.sum(p, axis=-1, keepdims=True)
        pb = p.astype(BF16)
        pv = jnp.dot(pb[:, 0:tq], v0_ref[0, :, hs], preferred_element_type=F32)
        pv = pv + jnp.dot(pb[:, tq:2 * tq], v1_ref[0, :, hs], preferred_element_type=F32)
        pv = pv + jnp.dot(pb[:, 2 * tq:3 * tq], v2_ref[0, :, hs], preferred_element_type=F32)
        obuf_ref[:, hs] = pv / denom
    oall = obuf_ref[...]
    r = lax.rsqrt(jnp.mean(oall * oall, axis=-1, keepdims=True) + EPS)
    o_ref[0] = (oall * r * g_ref[...]).astype(BF16)


def _na_attn(proj3, bias_pat, on_b):
    b, s, _ = proj3.shape
    tq = NA_QROWS * GRID_W
    nblk = s // tq

    def kvspec(col, off):
        return pl.BlockSpec((1, tq, WIDTH_B), lambda bi, i: (bi, jnp.clip(i - 1, 0, nblk - 3) + off, col))

    def pat(bi, i):
        return (jnp.where(i == 0, 0, jnp.where(i == nblk - 1, 2, 1)), 0, 0, 0)

    return pl.pallas_call(
        _na_kernel,
        out_shape=jax.ShapeDtypeStruct((b, s, WIDTH_B), BF16),
        grid=(b, nblk),
        in_specs=[
            pl.BlockSpec((1, tq, WIDTH_B), lambda bi, i: (bi, i, 1)),
            kvspec(2, 0), kvspec(2, 1), kvspec(2, 2),
            kvspec(3, 0), kvspec(3, 1), kvspec(3, 2),
            pl.BlockSpec((1, HB, tq, 3 * tq), pat),
            pl.BlockSpec((1, WIDTH_B), lambda bi, i: (0, 0)),
        ],
        out_specs=pl.BlockSpec((1, tq, WIDTH_B), lambda bi, i: (bi, i, 0)),
        scratch_shapes=[pltpu.VMEM((tq, WIDTH_B), F32)],
        compiler_params=_cparams(("parallel", "arbitrary")),
        name="na_attn",
    )(proj3, proj3, proj3, proj3, proj3, proj3, proj3, bias_pat, on_b)


def _out_proj_kernel(oa_ref, ob_ref, wo_ref, x_ref, g_ref, wr_ref, x1_ref, hp_ref, lg_ref):
    tm = x_ref.shape[0]
    half = D_MODEL // 2
    acc = jnp.dot(oa_ref[...], wo_ref[0:WIDTH_A, :], preferred_element_type=F32)
    acc = acc + jnp.dot(ob_ref[...], wo_ref[WIDTH_A:WIDTH_A + WIDTH_B, :], preferred_element_type=F32)
    x1 = x_ref[...] + acc
    x1_ref[...] = x1
    h32 = x1 * lax.rsqrt(jnp.mean(x1 * x1, axis=-1, keepdims=True) + EPS) * g_ref[...]
    h_hi = h32.astype(BF16)
    h_hi32 = h_hi.astype(F32)
    bits = pltpu.bitcast(h_hi32, U32)
    words = (bits[:, 0:half] >> 16) | (bits[:, half:D_MODEL] & jnp.uint32(0xFFFF0000))
    nwords = half // LANES
    for q in range(nwords):
        hp_ref[pl.ds(q, tm, stride=nwords), :] = words[:, q * LANES:(q + 1) * LANES]
    h_lo = (h32 - h_hi32).astype(BF16)
    r = jnp.dot(jnp.concatenate([h_hi, h_lo], axis=0), wr_ref[...], preferred_element_type=F32)
    ne = N_EXPERTS
    lg_ref[...] = (r[0:tm, 0:ne] + r[0:tm, ne:2 * ne]) + (r[tm:2 * tm, 0:ne] + r[tm:2 * tm, ne:2 * ne])


def _out_proj(oa, ob, wo, x2, g_ffn, wr2, tm):
    n = x2.shape[0]
    nwords = D_MODEL // 2 // LANES
    return pl.pallas_call(
        _out_proj_kernel,
        out_shape=(jax.ShapeDtypeStruct((n, D_MODEL), F32),
                   jax.ShapeDtypeStruct((n * nwords, LANES), U32),
                   jax.ShapeDtypeStruct((n, N_EXPERTS), F32)),
        grid=(n // tm,),
        in_specs=[
            pl.BlockSpec((tm, WIDTH_A), lambda i: (i, 0)),
            pl.BlockSpec((tm, WIDTH_B), lambda i: (i, 0)),
            pl.BlockSpec((WIDTH_A + WIDTH_B, D_MODEL), lambda i: (0, 0)),
            pl.BlockSpec((tm, D_MODEL), lambda i: (i, 0)),
            pl.BlockSpec((1, D_MODEL), lambda i: (0, 0)),
            pl.BlockSpec((D_MODEL, 2 * N_EXPERTS), lambda i: (0, 0)),
        ],
        out_specs=(pl.BlockSpec((tm, D_MODEL), lambda i: (i, 0)),
                   pl.BlockSpec((tm * nwords, LANES), lambda i: (i, 0)),
                   pl.BlockSpec((tm, N_EXPERTS), lambda i: (i, 0))),
        compiler_params=_cparams(("parallel",)),
        name="out_proj",
    )(oa, ob, wo, x2, g_ffn, wr2)


def _route_kernel(lg_ref, idx_ref, gate_ref, aff_ref, *, cap):
    ne, rr, _ = lg_ref.shape
    lg = lg_ref[...]
    ex = jnp.exp(lg - jnp.max(lg, axis=0, keepdims=True))
    aff_ref[...] = ex / jnp.sum(ex, axis=0, keepdims=True)

    upper = (lax.broadcasted_iota(I32, (LANES, LANES), 0) <= lax.broadcasted_iota(I32, (LANES, LANES), 1)).astype(BF16)
    lower = (lax.broadcasted_iota(I32, (rr, rr), 1) < lax.broadcasted_iota(I32, (rr, rr), 0)).astype(BF16)
    pcol = lax.broadcasted_iota(I32, (1, cap), 1).astype(F32)
    rowid = lax.broadcasted_iota(I32, (rr, 1), 0).astype(F32)
    laneid = lax.broadcasted_iota(I32, (LANES, 1), 0).astype(F32)

    def total(v):
        return jnp.sum(jnp.sum(v, axis=1, keepdims=True), axis=0, keepdims=True)

    def prefix(mask):
        within = jnp.dot(mask.astype(BF16), upper, preferred_element_type=F32)
        tot = within[:, LANES - 1:LANES]
        off = jnp.dot(lower, jnp.broadcast_to(tot, (rr, LANES)).astype(BF16), preferred_element_type=F32)
        return within + off, off[:, 0:1], tot

    def gather_rows(vals_t, onehot_t):
        return jnp.dot(vals_t.astype(BF16), onehot_t, preferred_element_type=F32)

    def body(e, carry):
        a = aff_ref[e]
        bits = pltpu.bitcast(a, I32)
        thr = jnp.zeros((1, 1), I32)
        for bit in range(30, -1, -1):
            cand = thr | jnp.int32(1 << bit)
            cnt = total((bits >= cand).astype(F32))
            thr = jnp.where(cnt >= cap, cand, thr)
        gt = bits > thr
        eq = bits == thr
        need = cap - total(gt.astype(F32))
        eq_rank, _, _ = prefix(eq.astype(F32))
        sel = gt | (eq & (eq_rank <= need))
        pos, excl, tot = prefix(sel.astype(F32))
        onehot_t = ((excl <= pcol) & (pcol < excl + tot))
        oh = onehot_t.astype(BF16)
        row_of = jnp.sum(jnp.where(onehot_t, rowid, 0.0), axis=0, keepdims=True)
        pos_t = pos.T
        pos_hi = jnp.floor(pos_t * (1.0 / 64.0))
        pos_lo = pos_t - 64.0 * pos_hi
        prow = 64.0 * gather_rows(pos_hi, oh) + gather_rows(pos_lo, oh)
        local = jnp.sum((prow <= pcol).astype(F32), axis=0, keepdims=True)
        idx_ref[e] = (row_of * float(LANES) + local).astype(I32)
        a_t = a.T
        a1 = a_t.astype(BF16).astype(F32)
        a2 = (a_t - a1).astype(BF16).astype(F32)
        a3 = a_t - a1 - a2
        arow = gather_rows(a1, oh) + gather_rows(a2, oh) + gather_rows(a3, oh)
        gate_ref[e] = jnp.sum(jnp.where(laneid == local, arow, 0.0), axis=0, keepdims=True)
        return carry

    lax.fori_loop(0, ne, body, 0)


def _route(lg3, cap):
    ne, rr, _ = lg3.shape
    return pl.pallas_call(
        functools.partial(_route_kernel, cap=cap),
        out_shape=(jax.ShapeDtypeStruct((ne, 1, cap), I32), jax.ShapeDtypeStruct((ne, 1, cap), F32)),
        scratch_shapes=[pltpu.VMEM((ne, rr, LANES), F32)],
        compiler_params=pltpu.CompilerParams(vmem_limit_bytes=VMEM_LIMIT),
        name="route",
    )(lg3)


def _ffn_kernel(idxp_ref, idx_ref, idxn_ref, gate_ref, hp_hbm, y_in_hbm, wg_ref, wu_ref, wd_ref, y_hbm,
                gbuf, xbuf, acc, ybuf, sem_h, sem_yi, sem_yo):
    del y_in_hbm
    tm = xbuf.shape[0]
    half = D_MODEL // 2
    nwords = half // LANES
    e = pl.program_id(0)
    c = pl.program_id(2)
    nj = pl.num_programs(1)
    nc = pl.num_programs(2)
    step = e * nj + pl.program_id(1)
    last = pl.num_programs(0) * nj - 1
    slot = lax.rem(step, 2)

    def h_copy(ids, r, sl):
        return pltpu.make_async_copy(hp_hbm.at[ids[0, 0, r]], gbuf.at[sl, pl.ds(r * nwords, nwords), :], sem_h.at[sl])

    def yi_copy(ids, r):
        return pltpu.make_async_copy(y_hbm.at[pl.ds(ids[0, 0, r], 1), :], ybuf.at[pl.ds(r, 1), :], sem_yi)

    def yo_copy(ids, r):
        return pltpu.make_async_copy(ybuf.at[pl.ds(r, 1), :], y_hbm.at[pl.ds(ids[0, 0, r], 1), :], sem_yo)

    def for_rows(fn):
        def body(r, carry):
            fn(r)
            return carry
        lax.fori_loop(0, tm, body, 0, unroll=8)

    @pl.when(c == 0)
    def _():
        @pl.when(step == 0)
        def _():
            for_rows(lambda r: h_copy(idx_ref, r, slot).start())

        for_rows(lambda r: h_copy(idx_ref, r, slot).wait())
        for q in range(nwords):
            w = gbuf[slot, pl.ds(q, tm, stride=nwords), :]
            xbuf[:, q * LANES:(q + 1) * LANES] = pltpu.bitcast(w << 16, F32).astype(BF16)
            xbuf[:, half + q * LANES:half + (q + 1) * LANES] = (
                pltpu.bitcast(w & jnp.uint32(0xFFFF0000), F32).astype(BF16))
        acc[...] = jnp.zeros_like(acc)

    @pl.when(c == 1)
    def _():
        @pl.when(step > 0)
        def _():
            for_rows(lambda r: yo_copy(idxp_ref, r).wait())

        for_rows(lambda r: yi_copy(idx_ref, r).start())

        @pl.when(step < last)
        def _():
            for_rows(lambda r: h_copy(idxn_ref, r, 1 - slot).start())

    x = xbuf[...]
    g = jnp.dot(x, wg_ref[0], preferred_element_type=F32)
    u = jnp.dot(x, wu_ref[0], preferred_element_type=F32)
    hid = (g * jax.nn.sigmoid(g)) * u
    acc[...] += jnp.dot(hid.astype(BF16), wd_ref[0], preferred_element_type=F32)

    @pl.when(c == nc - 1)
    def _():
        for_rows(lambda r: yi_copy(idx_ref, r).wait())
        onehot = lax.broadcasted_iota(I32, (1, N_EXPERTS), 1) == e
        gcol = jnp.sum(jnp.where(onehot, gate_ref[...], 0.0), axis=1, keepdims=True)
        ybuf[...] = ybuf[...] + acc[...] * gcol
        for_rows(lambda r: yo_copy(idx_ref, r).start())

        @pl.when(step == last)
        def _():
            for_rows(lambda r: yo_copy(idx_ref, r).wait())


def _ffn(idx3, gate_t, hp3, x1, wg, wu, wd, tm, tf):
    ne = N_EXPERTS
    cap = gate_t.shape[0]
    nj = cap // tm
    nc = EXPERT_FF // tf
    assert nc >= 3, "the row-traffic schedule uses ff chunks 0, 1 and the last one"
    n = x1.shape[0]
    nsteps = ne * nj
    nwords = D_MODEL // 2 // LANES

    def ids(off):
        return pl.BlockSpec((1, 1, tm), lambda e, j, c: (jnp.clip(e * nj + j + off, 0, nsteps - 1), 0, 0),
                            memory_space=pltpu.SMEM)

    return pl.pallas_call(
        _ffn_kernel,
        out_shape=jax.ShapeDtypeStruct((n, D_MODEL), F32),
        grid=(ne, nj, nc),
        in_specs=[
            ids(-1), ids(0), ids(1),
            pl.BlockSpec((tm, ne), lambda e, j, c: (j, 0)),
            pl.BlockSpec(memory_space=pl.ANY),
            pl.BlockSpec(memory_space=pl.ANY),
            pl.BlockSpec((1, D_MODEL, tf), lambda e, j, c: (e, 0, c)),
            pl.BlockSpec((1, D_MODEL, tf), lambda e, j, c: (e, 0, c)),
            pl.BlockSpec((1, tf, D_MODEL), lambda e, j, c: (e, c, 0)),
        ],
        out_specs=pl.BlockSpec(memory_space=pl.ANY),
        scratch_shapes=[
            pltpu.VMEM((2, tm * nwords, LANES), U32),
            pltpu.VMEM((tm, D_MODEL), BF16),
            pltpu.VMEM((tm, D_MODEL), F32),
            pltpu.VMEM((tm, D_MODEL), F32),
            pltpu.SemaphoreType.DMA((2,)),
            pltpu.SemaphoreType.DMA(()),
            pltpu.SemaphoreType.DMA(()),
        ],
        input_output_aliases={5: 0},
        compiler_params=_cparams(("arbitrary", "arbitrary", "arbitrary")),
        name="ffn",
    )(idx3, idx3, idx3, gate_t, hp3, x1, wg, wu, wd)


def _t5_bucket(rel):
    nb = N_BUCKETS // 2
    max_exact = nb // 2
    ret = np.where(rel > 0, nb, 0)
    n = np.abs(rel)
    nf = np.maximum(n, 1).astype(np.float32)
    large = max_exact + (np.log(nf / max_exact) / np.log(MAX_DISTANCE / max_exact) * (nb - max_exact)).astype(np.int32)
    large = np.minimum(large, nb - 1)
    return (ret + np.where(n < max_exact, n, large)).astype(np.int32)


def _win_bias(t5_table):
    rel = np.arange(-WINDOW, WINDOW + 1)
    vals = t5_table[_t5_bucket(rel)].astype(F32).T
    pad = jnp.full((HA, BLOCK - 1), NEG, F32)
    u = jnp.concatenate([pad, vals, pad], axis=1)
    return jnp.stack([u[:, BLOCK - 1 - il:BLOCK - 1 - il + 3 * BLOCK] for il in range(BLOCK)], axis=1)


def _na_bias(rpb, rows):
    ndr, ndc = 2 * NA_ROWS - 1, 2 * NA_COLS - 1
    pad = jnp.full((HB, ndr, GRID_W - NA_COLS), NEG, F32)
    u = jnp.concatenate([pad, rpb.astype(F32), pad], axis=2)
    col = jnp.stack([u[:, :, GRID_W - 1 - qc:2 * GRID_W - 1 - qc] for qc in range(GRID_W)], axis=2)
    qc = np.arange(GRID_W)[:, None]
    kc = np.arange(GRID_W)[None, :]
    cs = np.clip(qc - NA_COLS // 2, 0, GRID_W - NA_COLS)
    col = jnp.where((kc >= cs) & (kc < cs + NA_COLS), col, NEG)
    masked = jnp.full((HB, GRID_W, GRID_W), NEG, F32)
    nblk = rows // NA_QROWS
    pats = []
    for blk in (0, 1, nblk - 1):
        r0 = blk * NA_QROWS
        ws = int(np.clip(r0 - NA_ROWS // 2, 0, rows - NA_KROWS))
        qrows = []
        for qi in range(NA_QROWS):
            qrow = r0 + qi
            rs = int(np.clip(qrow - NA_ROWS // 2, 0, rows - NA_ROWS))
            blocks = []
            for kj in range(NA_KROWS):
                krow = ws + kj
                blocks.append(col[:, krow - qrow + NA_ROWS - 1] if rs <= krow < rs + NA_ROWS else masked)
            qrows.append(jnp.concatenate(blocks, axis=-1))
        pats.append(jnp.concatenate(qrows, axis=-2))
    return jnp.stack(pats, axis=0)


def _prepare(t5_table, norm_attn, w_in, q_norm_a, k_norm_a, q_norm_b, k_norm_b, sink_a, rpb_b,
             out_norm_a, out_norm_b, w_out, norm_ffn, w_router, w_gate, w_up, w_down):
    qa, ka, va = WIDTH_A, KVA * HEAD_DIM, KVA * HEAD_DIM
    o_ka, o_va, o_qb = qa, qa + ka, qa + ka + va
    o_kb, o_vb = o_qb + WIDTH_B, o_qb + 2 * WIDTH_B
    w_perm = jnp.concatenate([w_in[:, 0:qa], w_in[:, o_qb:o_kb], w_in[:, o_kb:o_vb], w_in[:, o_vb:],
                              w_in[:, o_ka:o_va], w_in[:, o_va:o_qb]], axis=1).astype(BF16)
    scale = HEAD_DIM ** -0.5
    ones = lambda k: jnp.ones((k,), F32)
    zeros = lambda k: jnp.zeros((k,), F32)
    gain = jnp.concatenate([jnp.tile(q_norm_a, HA) * scale, jnp.tile(q_norm_b, HB) * scale, jnp.tile(k_norm_b, HB),
                            ones(WIDTH_B), jnp.tile(k_norm_a, KVA), ones(va)])[None, :]
    flag = jnp.concatenate([ones(qa), ones(WIDTH_B), ones(WIDTH_B), zeros(WIDTH_B), ones(ka), zeros(va)])[None, :]

    bias_st = _win_bias(t5_table).reshape(KVA, GA * BLOCK, 3 * BLOCK)
    sink_st = jnp.repeat(sink_a.astype(F32), BLOCK).reshape(KVA, GA * BLOCK, 1)

    wr_hi = w_router.astype(BF16)
    wr_lo = (w_router - wr_hi.astype(F32)).astype(BF16)
    return dict(
        w_perm=w_perm, gain=gain, flag=flag, g_attn=norm_attn[None, :], bias_st=bias_st, sink_st=sink_st,
        rpb=rpb_b, on_a=out_norm_a[None, :], on_b=out_norm_b[None, :],
        wo=w_out.astype(BF16), g_ffn=norm_ffn[None, :], wr2=jnp.concatenate([wr_hi, wr_lo], axis=1),
        wg=w_gate.astype(BF16), wu=w_up.astype(BF16), wd=w_down.astype(BF16))


def _pick(n, pref):
    t = pref
    while n % t:
        t //= 2
    return t


def _trunk(x, p):
    b, s, _ = x.shape
    n = b * s
    x2 = x.reshape(n, D_MODEL)
    proj = _in_proj(x2, p["g_attn"], p["w_perm"], p["gain"], p["flag"], _pick(n, 1024), 512)
    proj3 = proj.reshape(b, s, PROJ_WIDTH)
    oa = _win_attn(proj3, p["bias_st"], p["sink_st"], p["on_a"], _pick(s // BLOCK, 4))
    ob = _na_attn(proj3, _na_bias(p["rpb"], s // GRID_W), p["on_b"])
    x1, hp, logits = _out_proj(oa.reshape(n, WIDTH_A), ob.reshape(n, WIDTH_B), p["wo"], x2, p["g_ffn"], p["wr2"],
                               _pick(n, 256))
    cap =CAPACITY_FACTOR * n // N_EXPERTS
    idx, gate = _route(logits.T.reshape(N_EXPERTS, n // LANES, LANES), cap)
    tm = _pick(cap, 512)
    hp3 = hp.reshape(n, D_MODEL // 2 // LANES, LANES)
    y = _ffn(idx.reshape(N_EXPERTS * (cap // tm), 1, tm), gate.reshape(N_EXPERTS, cap).T, hp3, x1,
             p["wg"], p["wu"], p["wd"], tm, 512)
    return y.reshape(b, s, D_MODEL)


def kernel(x_prompt, x_sample, t5_table, norm_attn, w_in, q_norm_a, k_norm_a, q_norm_b, k_norm_b, sink_a, rpb_b,
           out_norm_a, out_norm_b, w_out, norm_ffn, w_router, w_gate, w_up, w_down):
    p = _prepare(t5_table, norm_attn[0], w_in[0], q_norm_a[0], k_norm_a[0], q_norm_b[0], k_norm_b[0], sink_a[0],
                 rpb_b[0], out_norm_a[0], out_norm_b[0], w_out[0], norm_ffn[0], w_router[0], w_gate[0], w_up[0],
                 w_down[0])
    return (_trunk(x_prompt, p), _trunk(x_sample, p))
```

```python
import functools

import numpy as np
import jax
import jax.numpy as jnp
from jax import lax
from jax.experimental import pallas as pl
from jax.experimental.pallas import tpu as pltpu

F32 = jnp.float32
BF16 = jnp.bfloat16
I32 = jnp.int32
U32 = jnp.uint32

D_MODEL = 2048
HEAD_DIM = 128
HA = 8
KVA = 2
GA = HA // KVA
HB = 8
WIDTH_A = HA * HEAD_DIM
WIDTH_B = HB * HEAD_DIM
PROJ_WIDTH = WIDTH_A + 2 * KVA * HEAD_DIM + 3 * WIDTH_B
WINDOW = 128
BLOCK = 128
N_BUCKETS = 32
MAX_DISTANCE = 128
GRID_W = 64
NA_ROWS = 8
NA_COLS = 16
NA_QROWS = 4
NA_KROWS = 3 * NA_QROWS
N_EXPERTS = 16
EXPERT_FF = D_MODEL
CAPACITY_FACTOR = 2
EPS = 1e-6
NEG = -1e30
LANES = 128
VMEM_LIMIT = 56 * 1024 * 1024


def _cparams(sem, vmem=VMEM_LIMIT):
    return pltpu.CompilerParams(dimension_semantics=sem, vmem_limit_bytes=vmem)


def _dot_nt(a, b):
    return lax.dot_general(a, b, (((1,), (1,)), ((), ())), preferred_element_type=F32)


def _in_proj_kernel(x_ref, g_ref, w_ref, gain_ref, flag_ref, o_ref, xn_ref):
    @pl.when(pl.program_id(1) == 0)
    def _():
        x = x_ref[...]
        ms = jnp.mean(x * x, axis=-1, keepdims=True)
        xn_ref[...] = (x * lax.rsqrt(ms + EPS) * g_ref[...]).astype(BF16)

    acc = jnp.dot(xn_ref[...], w_ref[...], preferred_element_type=F32)
    for h in range(acc.shape[1] // HEAD_DIM):
        sl = slice(h * HEAD_DIM, (h + 1) * HEAD_DIM)
        a = acc[:, sl]
        r = lax.rsqrt(jnp.mean(a * a, axis=-1, keepdims=True) + EPS)
        scale = jnp.where(flag_ref[:, sl] > 0.0, r, 1.0) * gain_ref[:, sl]
        o_ref[:, sl] = (a * scale).astype(BF16)


def _in_proj(x2, g_attn, w_perm, gain, flag, tm, tn):
    n = x2.shape[0]
    return pl.pallas_call(
        _in_proj_kernel,
        out_shape=jax.ShapeDtypeStruct((n, PROJ_WIDTH), BF16),
        grid=(n // tm, PROJ_WIDTH // tn),
        in_specs=[
            pl.BlockSpec((tm, D_MODEL), lambda i, j: (i, 0)),
            pl.BlockSpec((1, D_MODEL), lambda i, j: (0, 0)),
            pl.BlockSpec((D_MODEL, tn), lambda i, j: (0, j)),
            pl.BlockSpec((1, tn), lambda i, j: (0, j)),
            pl.BlockSpec((1, tn), lambda i, j: (0, j)),
        ],
        out_specs=pl.BlockSpec((tm, tn), lambda i, j: (i, j)),
        scratch_shapes=[pltpu.VMEM((tm, D_MODEL), BF16)],
        compiler_params=_cparams(("parallel", "arbitrary")),
        name="in_proj",
    )(x2, g_attn, w_perm, gain, flag)


def _win_kernel(q_ref, kp_ref, kc_ref, kn_ref, vp_ref, vc_ref, vn_ref, bias_ref, sink_ref, g_ref,
                o_ref, obuf_ref, s_ref):
    nsub = q_ref.shape[1] // BLOCK
    i = pl.program_id(1)
    seq = pl.num_programs(1) * nsub * BLOCK
    lane_pos = lax.broadcasted_iota(I32, (1, 3 * BLOCK), 1) - BLOCK

    def window(prev_ref, cur_ref, next_ref, sb, ks):
        lo, hi = (sb - 1) * BLOCK, (sb + 2) * BLOCK
        parts = []
        if lo < 0:
            parts.append(prev_ref[0, :, ks])
        parts.append(cur_ref[0, max(lo, 0):min(hi, nsub * BLOCK), ks])
        if hi > nsub * BLOCK:
            parts.append(next_ref[0, :, ks])
        return jnp.concatenate(parts, axis=0)

    chains = [(sb, kh) for sb in range(nsub) for kh in range(KVA)]
    for ci, (sb, kh) in enumerate(chains):
        rows = slice(sb * BLOCK, (sb + 1) * BLOCK)
        ks = slice(kh * HEAD_DIM, (kh + 1) * HEAD_DIM)
        qs = jnp.concatenate(
            [q_ref[0, rows, (kh * GA + g) * HEAD_DIM:(kh * GA + g + 1) * HEAD_DIM] for g in range(GA)], axis=0)
        s_ref[ci] = _dot_nt(qs, window(kp_ref, kc_ref, kn_ref, sb, ks))
    for ci, (sb, kh) in enumerate(chains):
        rows = slice(sb * BLOCK, (sb + 1) * BLOCK)
        ks = slice(kh * HEAD_DIM, (kh + 1) * HEAD_DIM)
        kpos = (i * nsub + sb) * BLOCK + lane_pos
        pen = jnp.where((kpos >= 0) & (kpos < seq), 0.0, NEG).astype(F32)
        s = s_ref[ci] + bias_ref[kh] + pen
        sk = sink_ref[kh]
        m = jnp.maximum(jnp.max(s, axis=-1, keepdims=True), sk)
        p = jnp.exp(s - m)
        denom = jnp.sum(p, axis=-1, keepdims=True) + jnp.exp(sk - m)
        vw = window(vp_ref, vc_ref, vn_ref, sb, ks)
        o = jnp.dot(p.astype(BF16), vw, preferred_element_type=F32) / denom
        for g in range(GA):
            hs = slice((kh * GA + g) * HEAD_DIM, (kh * GA + g + 1) * HEAD_DIM)
            obuf_ref[rows, hs] = o[g * BLOCK:(g + 1) * BLOCK, :]
    oall = obuf_ref[...]
    r = lax.rsqrt(jnp.mean(oall * oall, axis=-1, keepdims=True) + EPS)
    o_ref[0] = (oall * r * g_ref[...]).astype(BF16)


def _win_attn(proj3, bias_st, sink_st, on_a, nsub):
    b, s, _ = proj3.shape
    tq = nsub * BLOCK
    nb = s // BLOCK
    kcol = (WIDTH_A + 3 * WIDTH_B) // (KVA * HEAD_DIM)
    kw = KVA * HEAD_DIM

    def edge(col, side):
        if side < 0:
            return pl.BlockSpec((1, BLOCK, kw), lambda bi, i: (bi, jnp.maximum(i * nsub - 1, 0), col))
        return pl.BlockSpec((1, BLOCK, kw), lambda bi, i: (bi, jnp.minimum(i * nsub + nsub, nb - 1), col))

    def main(col):
        return pl.BlockSpec((1, tq, kw), lambda bi, i: (bi, i, col))

    return pl.pallas_call(
        _win_kernel,
        out_shape=jax.ShapeDtypeStruct((b, s, WIDTH_A), BF16),
        grid=(b, s // tq),
        in_specs=[
            pl.BlockSpec((1, tq, WIDTH_A), lambda bi, i: (bi, i, 0)),
            edge(kcol, -1), main(kcol), edge(kcol, 1),
            edge(kcol + 1, -1), main(kcol + 1), edge(kcol + 1, 1),
            pl.BlockSpec((KVA, GA * BLOCK, 3 * BLOCK), lambda bi, i: (0, 0, 0)),
            pl.BlockSpec((KVA, GA * BLOCK, 1), lambda bi, i: (0, 0, 0)),
            pl.BlockSpec((1, WIDTH_A), lambda bi, i: (0, 0)),
        ],
        out_specs=pl.BlockSpec((1, tq, WIDTH_A), lambda bi, i: (bi, i, 0)),
        scratch_shapes=[pltpu.VMEM((tq, WIDTH_A), F32),
                        pltpu.VMEM((nsub * KVA, GA * BLOCK, 3 * BLOCK), F32)],
        compiler_params=_cparams(("parallel", "arbitrary")),
        name="win_attn",
    )(proj3, proj3, proj3, proj3, proj3, proj3, proj3, bias_st, sink_st, on_a)


def _na_kernel(q_ref, k0_ref, k1_ref, k2_ref, v0_ref, v1_ref, v2_ref, bias_ref, g_ref, o_ref, obuf_ref, s_ref):
    tq = q_ref.shape[1]
    for h in range(HB):
        hs = slice(h * HEAD_DIM, (h + 1) * HEAD_DIM)
        qh = q_ref[0, :, hs]
        for kb, kr in enumerate((k0_ref, k1_ref, k2_ref)):
            s_ref[h, :, kb * tq:(kb + 1) * tq] = _dot_nt(qh, kr[0, :, hs])
    for h in range(HB):
        hs = slice(h * HEAD_DIM, (h + 1) * HEAD_DIM)
        s = s_ref[h] + bias_ref[0, h]
        m = jnp.max(s, axis=-1, keepdims=True)
        p = jnp.exp(s - m)
        denom = jnp.sum(p, axis=-1, keepdims=True)
        pb = p.astype(BF16)
        pv = jnp.dot(pb[:, 0:tq], v0_ref[0, :, hs], preferred_element_type=F32)
        pv = pv + jnp.dot(pb[:, tq:2 * tq], v1_ref[0, :, hs], preferred_element_type=F32)
        pv = pv + jnp.dot(pb[:, 2 * tq:3 * tq], v2_ref[0, :, hs], preferred_element_type=F32)
        obuf_ref[:, hs] = pv / denom
    oall = obuf_ref[...]
    r = lax.rsqrt(jnp.mean(oall * oall, axis=-1, keepdims=True) + EPS)
    o_ref[0] = (oall * r * g_ref[...]).astype(BF16)


def _na_attn(proj3, bias_pat, on_b):
    b, s, _ = proj3.shape
    tq = NA_QROWS * GRID_W
    nblk = s // tq

    def kvspec(col, off):
        return pl.BlockSpec((1, tq, WIDTH_B), lambda bi, i: (bi, jnp.clip(i - 1, 0, nblk - 3) + off, col))

    def pat(bi, i):
        return (jnp.where(i == 0, 0, jnp.where(i == nblk - 1, 2, 1)), 0, 0, 0)

    return pl.pallas_call(
        _na_kernel,
        out_shape=jax.ShapeDtypeStruct((b, s, WIDTH_B), BF16),
        grid=(b, nblk),
        in_specs=[
            pl.BlockSpec((1, tq, WIDTH_B), lambda bi, i: (bi, i, 1)),
            kvspec(2, 0), kvspec(2, 1), kvspec(2, 2),
            kvspec(3, 0), kvspec(3, 1), kvspec(3, 2),
            pl.BlockSpec((1, HB, tq, 3 * tq), pat),
            pl.BlockSpec((1, WIDTH_B), lambda bi, i: (0, 0)),
        ],
        out_specs=pl.BlockSpec((1, tq, WIDTH_B), lambda bi, i: (bi, i, 0)),
        scratch_shapes=[pltpu.VMEM((tq, WIDTH_B), F32), pltpu.VMEM((HB, tq, 3 * tq), F32)],
        compiler_params=_cparams(("parallel", "arbitrary")),
        name="na_attn",
    )(proj3, proj3, proj3, proj3, proj3, proj3, proj3, bias_pat, on_b)


def _out_proj_kernel(oa_ref, ob_ref, wo_ref, x_ref, g_ref, wr_ref, x1_ref, hp_ref, lg_ref):
    tm = x_ref.shape[0]
    half = D_MODEL // 2
    acc = jnp.dot(oa_ref[...], wo_ref[0:WIDTH_A, :], preferred_element_type=F32)
    acc = acc + jnp.dot(ob_ref[...], wo_ref[WIDTH_A:WIDTH_A + WIDTH_B, :], preferred_element_type=F32)
    x1 = x_ref[...] + acc
    x1_ref[...] = x1
    h32 = x1 * lax.rsqrt(jnp.mean(x1 * x1, axis=-1, keepdims=True) + EPS) * g_ref[...]
    h_hi = h32.astype(BF16)
    h_hi32 = h_hi.astype(F32)
    bits = pltpu.bitcast(h_hi32, U32)
    words = (bits[:, 0:half] >> 16) | (bits[:, half:D_MODEL] & jnp.uint32(0xFFFF0000))
    nwords = half // LANES
    for q in range(nwords):
        hp_ref[pl.ds(q, tm, stride=nwords), :] = words[:, q * LANES:(q + 1) * LANES]
    h_lo = (h32 - h_hi32).astype(BF16)
    r = jnp.dot(jnp.concatenate([h_hi, h_lo], axis=0), wr_ref[...], preferred_element_type=F32)
    ne = N_EXPERTS
    lg_ref[...] = (r[0:tm, 0:ne] + r[0:tm, ne:2 * ne]) + (r[tm:2 * tm, 0:ne] + r[tm:2 * tm, ne:2 * ne])


def _out_proj(oa, ob, wo, x2, g_ffn, wr2, tm):
    n = x2.shape[0]
    nwords = D_MODEL // 2 // LANES
    return pl.pallas_call(
        _out_proj_kernel,
        out_shape=(jax.ShapeDtypeStruct((n, D_MODEL), F32),
                   jax.ShapeDtypeStruct((n * nwords, LANES), U32),
                   jax.ShapeDtypeStruct((n, N_EXPERTS), F32)),
        grid=(n // tm,),
        in_specs=[
            pl.BlockSpec((tm, WIDTH_A), lambda i: (i, 0)),
            pl.BlockSpec((tm, WIDTH_B), lambda i: (i, 0)),
            pl.BlockSpec((WIDTH_A + WIDTH_B, D_MODEL), lambda i: (0, 0)),
            pl.BlockSpec((tm, D_MODEL), lambda i: (i, 0)),
            pl.BlockSpec((1, D_MODEL), lambda i: (0, 0)),
            pl.BlockSpec((D_MODEL, 2 * N_EXPERTS), lambda i: (0, 0)),
        ],
        out_specs=(pl.BlockSpec((tm, D_MODEL), lambda i: (i, 0)),
                   pl.BlockSpec((tm * nwords, LANES), lambda i: (i, 0)),
                   pl.BlockSpec((tm, N_EXPERTS), lambda i: (i, 0))),
        compiler_params=_cparams(("parallel",)),
        name="out_proj",
    )(oa, ob, wo, x2, g_ffn, wr2)


def _route_kernel(lg_ref, idx_ref, gate_ref, aff_ref, *, cap):
    ne, rr, _ = lg_ref.shape
    lg = lg_ref[...]
    ex = jnp.exp(lg - jnp.max(lg, axis=0, keepdims=True))
    aff_ref[...] = ex / jnp.sum(ex, axis=0, keepdims=True)

    upper = (lax.broadcasted_iota(I32, (LANES, LANES), 0) <= lax.broadcasted_iota(I32, (LANES, LANES), 1)).astype(BF16)
    lower = (lax.broadcasted_iota(I32, (rr, rr), 1) < lax.broadcasted_iota(I32, (rr, rr), 0)).astype(BF16)
    pcol = lax.broadcasted_iota(I32, (1, cap), 1).astype(F32)
    rowid = lax.broadcasted_iota(I32, (rr, 1), 0).astype(F32)
    laneid = lax.broadcasted_iota(I32, (LANES, 1), 0).astype(F32)

    def total(v):
        return jnp.sum(jnp.sum(v, axis=1, keepdims=True), axis=0, keepdims=True)

    def prefix(mask):
        within = jnp.dot(mask.astype(BF16), upper, preferred_element_type=F32)
        tot = within[:, LANES - 1:LANES]
        off = jnp.dot(lower, jnp.broadcast_to(tot, (rr, LANES)).astype(BF16), preferred_element_type=F32)
        return within + off, off[:, 0:1], tot

    def gather_rows(vals_t, onehot_t):
        return jnp.dot(vals_t.astype(BF16), onehot_t, preferred_element_type=F32)

    def body(e, carry):
        a = aff_ref[e]
        bits = pltpu.bitcast(a, I32)
        thr = jnp.zeros((1, 1), I32)
        for bit in range(30, -1, -1):
            cand = thr | jnp.int32(1 << bit)
            cnt = total((bits >= cand).astype(F32))
            thr = jnp.where(cnt >= cap, cand, thr)
        gt = bits > thr
        eq = bits == thr
        need = cap - total(gt.astype(F32))
        eq_rank, _, _ = prefix(eq.astype(F32))
        sel = gt | (eq & (eq_rank <= need))
        pos, excl, tot = prefix(sel.astype(F32))
        onehot_t = ((excl <= pcol) & (pcol < excl + tot))
        oh = onehot_t.astype(BF16)
        row_of = jnp.sum(jnp.where(onehot_t, rowid, 0.0), axis=0, keepdims=True)
        pos_t = pos.T
        pos_hi = jnp.floor(pos_t * (1.0 / 64.0))
        pos_lo = pos_t - 64.0 * pos_hi
        prow = 64.0 * gather_rows(pos_hi, oh) + gather_rows(pos_lo, oh)
        local = jnp.sum((prow <= pcol).astype(F32), axis=0, keepdims=True)
        idx_ref[e] = (row_of * float(LANES) + local).astype(I32)
        a_t = a.T
        a1 = a_t.astype(BF16).astype(F32)
        a2 = (a_t - a1).astype(BF16).astype(F32)
        a3 = a_t - a1 - a2
        arow = gather_rows(a1, oh) + gather_rows(a2, oh) + gather_rows(a3, oh)
        gate_ref[e] = jnp.sum(jnp.where(laneid == local, arow, 0.0), axis=0, keepdims=True)
        return carry

    lax.fori_loop(0, ne, body, 0)


def _route(lg3, cap):
    ne, rr, _ = lg3.shape
    return pl.pallas_call(
        functools.partial(_route_kernel, cap=cap),
        out_shape=(jax.ShapeDtypeStruct((ne, 1, cap), I32), jax.ShapeDtypeStruct((ne, 1, cap), F32)),
        scratch_shapes=[pltpu.VMEM((ne, rr, LANES), F32)],
        compiler_params=pltpu.CompilerParams(vmem_limit_bytes=VMEM_LIMIT),
        name="route",
    )(lg3)


def _ffn_kernel(idxp_ref, idx_ref, idxn_ref, gate_ref, hp_hbm, y_in_hbm, wg_ref, wu_ref, wd_ref, y_hbm,
                gbuf, xbuf, acc, ybuf, sem_h, sem_yi, sem_yo):
    del y_in_hbm
    tm = xbuf.shape[0]
    half = D_MODEL // 2
    nwords = half // LANES
    e = pl.program_id(0)
    c = pl.program_id(2)
    nj = pl.num_programs(1)
    nc = pl.num_programs(2)
    step = e * nj + pl.program_id(1)
    last = pl.num_programs(0) * nj - 1
    slot = lax.rem(step, 2)

    def h_copy(ids, r, sl):
        return pltpu.make_async_copy(hp_hbm.at[ids[0, 0, r]], gbuf.at[sl, pl.ds(r * nwords, nwords), :], sem_h.at[sl])

    def yi_copy(ids, r):
        return pltpu.make_async_copy(y_hbm.at[pl.ds(ids[0, 0, r], 1), :], ybuf.at[pl.ds(r, 1), :], sem_yi)

    def yo_copy(ids, r):
        return pltpu.make_async_copy(ybuf.at[pl.ds(r, 1), :], y_hbm.at[pl.ds(ids[0, 0, r], 1), :], sem_yo)

    def for_rows(fn):
        def body(r, carry):
            fn(r)
            return carry
        lax.fori_loop(0, tm, body, 0, unroll=8)

    def start_rows(fn):
        for r in range(tm):
            fn(r).start()

    @pl.when(c == 0)
    def _():
        @pl.when(step == 0)
        def _():
            for_rows(lambda r: h_copy(idx_ref, r, slot).start())

        for_rows(lambda r: h_copy(idx_ref, r, slot).wait())
        for q in range(nwords):
            w = gbuf[slot, pl.ds(q, tm, stride=nwords), :]
            xbuf[:, q * LANES:(q + 1) * LANES] = pltpu.bitcast(w << 16, F32).astype(BF16)
            xbuf[:, half + q * LANES:half + (q + 1) * LANES] = (
                pltpu.bitcast(w & jnp.uint32(0xFFFF0000), F32).astype(BF16))
        acc[...] = jnp.zeros_like(acc)

    @pl.when(c == 1)
    def _():
        @pl.when(step > 0)
        def _():
            for_rows(lambda r: yo_copy(idxp_ref, r).wait())

        start_rows(lambda r: yi_copy(idx_ref, r))

        @pl.when(step < last)
        def _():
            start_rows(lambda r: h_copy(idxn_ref, r, 1 - slot))

    x = xbuf[...]
    g = jnp.dot(x, wg_ref[0], preferred_element_type=F32)
    u = jnp.dot(x, wu_ref[0], preferred_element_type=F32)
    hid = (g * jax.nn.sigmoid(g)) * u
    acc[...] += jnp.dot(hid.astype(BF16), wd_ref[0], preferred_element_type=F32)

    @pl.when(c == nc - 1)
    def _():
        for_rows(lambda r: yi_copy(idx_ref, r).wait())
        onehot = lax.broadcasted_iota(I32, (1, N_EXPERTS), 1) == e
        gcol = jnp.sum(jnp.where(onehot, gate_ref[...], 0.0), axis=1, keepdims=True)
        ybuf[...] = ybuf[...] + acc[...] * gcol
        start_rows(lambda r: yo_copy(idx_ref, r))

        @pl.when(step == last)
        def _():
            for_rows(lambda r: yo_copy(idx_ref, r).wait())


def _ffn(idx3, gate_t, hp3, x1, wg, wu, wd, tm, tf):
    ne = N_EXPERTS
    cap = gate_t.shape[0]
    nj = cap // tm
    nc = EXPERT_FF // tf
    assert nc >= 3, "the row-traffic schedule uses ff chunks 0, 1 and the last one"
    n = x1.shape[0]
    nsteps = ne * nj
    nwords = D_MODEL // 2 // LANES

    def ids(off):
        return pl.BlockSpec((1, 1, tm), lambda e, j, c: (jnp.clip(e * nj + j + off, 0, nsteps - 1), 0, 0),
                            memory_space=pltpu.SMEM)

    return pl.pallas_call(
        _ffn_kernel,
        out_shape=jax.ShapeDtypeStruct((n, D_MODEL), F32),
        grid=(ne, nj, nc),
        in_specs=[
            ids(-1), ids(0), ids(1),
            pl.BlockSpec((tm, ne), lambda e, j, c: (j, 0)),
            pl.BlockSpec(memory_space=pl.ANY),
            pl.BlockSpec(memory_space=pl.ANY),
            pl.BlockSpec((1, D_MODEL, tf), lambda e, j, c: (e, 0, c)),
            pl.BlockSpec((1, D_MODEL, tf), lambda e, j, c: (e, 0, c)),
            pl.BlockSpec((1, tf, D_MODEL), lambda e, j, c: (e, c, 0)),
        ],
        out_specs=pl.BlockSpec(memory_space=pl.ANY),
        scratch_shapes=[
            pltpu.VMEM((2, tm * nwords, LANES), U32),
            pltpu.VMEM((tm, D_MODEL), BF16),
            pltpu.VMEM((tm, D_MODEL), F32),
            pltpu.VMEM((tm, D_MODEL), F32),
            pltpu.SemaphoreType.DMA((2,)),
            pltpu.SemaphoreType.DMA(()),
            pltpu.SemaphoreType.DMA(()),
        ],
        input_output_aliases={5: 0},
        compiler_params=_cparams(("arbitrary", "arbitrary", "arbitrary")),
        name="ffn",
    )(idx3, idx3, idx3, gate_t, hp3, x1, wg, wu, wd)


def _t5_bucket(rel):
    nb = N_BUCKETS // 2
    max_exact = nb // 2
    ret = np.where(rel > 0, nb, 0)
    n = np.abs(rel)
    nf = np.maximum(n, 1).astype(np.float32)
    large = max_exact + (np.log(nf / max_exact) / np.log(MAX_DISTANCE / max_exact) * (nb - max_exact)).astype(np.int32)
    large = np.minimum(large, nb - 1)
    return (ret + np.where(n < max_exact, n, large)).astype(np.int32)


def _win_bias(t5_table):
    rel = np.arange(-WINDOW, WINDOW + 1)
    vals = t5_table[_t5_bucket(rel)].astype(F32).T
    pad = jnp.full((HA, BLOCK - 1), NEG, F32)
    u = jnp.concatenate([pad, vals, pad], axis=1)
    return jnp.stack([u[:, BLOCK - 1 - il:BLOCK - 1 - il + 3 * BLOCK] for il in range(BLOCK)], axis=1)


def _na_bias(rpb, rows):
    ndr, ndc = 2 * NA_ROWS - 1, 2 * NA_COLS - 1
    pad = jnp.full((HB, ndr, GRID_W - NA_COLS), NEG, F32)
    u = jnp.concatenate([pad, rpb.astype(F32), pad], axis=2)
    col = jnp.stack([u[:, :, GRID_W - 1 - qc:2 * GRID_W - 1 - qc] for qc in range(GRID_W)], axis=2)
    qc = np.arange(GRID_W)[:, None]
    kc = np.arange(GRID_W)[None, :]
    cs = np.clip(qc - NA_COLS // 2, 0, GRID_W - NA_COLS)
    col = jnp.where((kc >= cs) & (kc < cs + NA_COLS), col, NEG)
    masked = jnp.full((HB, GRID_W, GRID_W), NEG, F32)
    nblk = rows // NA_QROWS
    pats = []
    for blk in (0, 1, nblk - 1):
        r0 = blk * NA_QROWS
        ws = int(np.clip(r0 - NA_ROWS // 2, 0, rows - NA_KROWS))
        qrows = []
        for qi in range(NA_QROWS):
            qrow = r0 + qi
            rs = int(np.clip(qrow - NA_ROWS // 2, 0, rows - NA_ROWS))
            blocks = []
            for kj in range(NA_KROWS):
                krow = ws + kj
                blocks.append(col[:, krow - qrow + NA_ROWS - 1] if rs <= krow < rs + NA_ROWS else masked)
            qrows.append(jnp.concatenate(blocks, axis=-1))
        pats.append(jnp.concatenate(qrows, axis=-2))
    return jnp.stack(pats, axis=0)


def _prepare(t5_table, norm_attn, w_in, q_norm_a, k_norm_a, q_norm_b, k_norm_b, sink_a, rpb_b,
             out_norm_a, out_norm_b, w_out, norm_ffn, w_router, w_gate, w_up, w_down):
    qa, ka, va = WIDTH_A, KVA * HEAD_DIM, KVA * HEAD_DIM
    o_ka, o_va, o_qb = qa, qa + ka, qa + ka + va
    o_kb, o_vb = o_qb + WIDTH_B, o_qb + 2 * WIDTH_B
    w_perm = jnp.concatenate([w_in[:, 0:qa], w_in[:, o_qb:o_kb], w_in[:, o_kb:o_vb], w_in[:, o_vb:],
                              w_in[:, o_ka:o_va], w_in[:, o_va:o_qb]], axis=1).astype(BF16)
    scale = HEAD_DIM ** -0.5
    ones = lambda k: jnp.ones((k,), F32)
    zeros = lambda k: jnp.zeros((k,), F32)
    gain = jnp.concatenate([jnp.tile(q_norm_a, HA) * scale, jnp.tile(q_norm_b, HB) * scale, jnp.tile(k_norm_b, HB),
                            ones(WIDTH_B), jnp.tile(k_norm_a, KVA), ones(va)])[None, :]
    flag = jnp.concatenate([ones(qa), ones(WIDTH_B), ones(WIDTH_B), zeros(WIDTH_B), ones(ka), zeros(va)])[None, :]

    bias_st = _win_bias(t5_table).reshape(KVA, GA * BLOCK, 3 * BLOCK)
    sink_st = jnp.repeat(sink_a.astype(F32), BLOCK).reshape(KVA, GA * BLOCK, 1)

    wr_hi = w_router.astype(BF16)
    wr_lo = (w_router - wr_hi.astype(F32)).astype(BF16)
    return dict(
        w_perm=w_perm, gain=gain, flag=flag, g_attn=norm_attn[None, :], bias_st=bias_st, sink_st=sink_st,
        rpb=rpb_b, on_a=out_norm_a[None, :], on_b=out_norm_b[None, :],
        wo=w_out.astype(BF16), g_ffn=norm_ffn[None, :], wr2=jnp.concatenate([wr_hi, wr_lo], axis=1),
        wg=w_gate.astype(BF16), wu=w_up.astype(BF16), wd=w_down.astype(BF16))


def _pick(n, pref):
    t = pref
    while n % t:
        t //= 2
    return t


def _trunk(x, p):
    b, s, _ = x.shape
    n = b * s
    x2 = x.reshape(n, D_MODEL)
    proj = _in_proj(x2, p["g_attn"], p["w_perm"], p["gain"], p["flag"], _pick(n, 1024), 512)
    proj3 = proj.reshape(b, s, PROJ_WIDTH)
    oa = _win_attn(proj3, p["bias_st"], p["sink_st"], p["on_a"], _pick(s // BLOCK, 4))
    ob = _na_attn(proj3, _na_bias(p["rpb"], s // GRID_W), p["on_b"])
    x1, hp, logits = _out_proj(oa.reshape(n, WIDTH_A), ob.reshape(n, WIDTH_B), p["wo"], x2, p["g_ffn"], p["wr2"],
                               _pick(n, 256))
    cap =CAPACITY_FACTOR * n // N_EXPERTS
    idx, gate = _route(logits.T.reshape(N_EXPERTS, n // LANES, LANES), cap)
    tm = _pick(cap, 512)
    hp3 = hp.reshape(n, D_MODEL // 2 // LANES, LANES)
    y = _ffn(idx.reshape(N_EXPERTS * (cap // tm), 1, tm), gate.reshape(N_EXPERTS, cap).T, hp3, x1,
             p["wg"], p["wu"], p["wd"], tm, 512)
    return y.reshape(b, s, D_MODEL)


def kernel(x_prompt, x_sample, t5_table, norm_attn, w_in, q_norm_a, k_norm_a, q_norm_b, k_norm_b, sink_a, rpb_b,
           out_norm_a, out_norm_b, w_out, norm_ffn, w_router, w_gate, w_up, w_down):
    p = _prepare(t5_table, norm_attn[0], w_in[0], q_norm_a[0], k_norm_a[0], q_norm_b[0], k_norm_b[0], sink_a[0],
                 rpb_b[0], out_norm_a[0], out_norm_b[0], w_out[0], norm_ffn[0], w_router[0], w_gate[0], w_up[0],
                 w_down[0])
    return (_trunk(x_prompt, p), _trunk(x_sample, p))
```

```python
import functools

import numpy as np
import jax
import jax.numpy as jnp
from jax import lax
from jax.experimental import pallas as pl
from jax.experimental.pallas import tpu as pltpu

F32 = jnp.float32
BF16 = jnp.bfloat16
I32 = jnp.int32
U32 = jnp.uint32

D_MODEL = 2048
HEAD_DIM = 128
HA = 8
KVA = 2
GA = HA // KVA
HB = 8
WIDTH_A = HA * HEAD_DIM
WIDTH_B = HB * HEAD_DIM
PROJ_WIDTH = WIDTH_A + 2 * KVA * HEAD_DIM + 3 * WIDTH_B
WINDOW = 128
BLOCK = 128
N_BUCKETS = 32
MAX_DISTANCE = 128
GRID_W = 64
NA_ROWS = 8
NA_COLS = 16
NA_QROWS = 4
NA_KROWS = 3 * NA_QROWS
N_EXPERTS = 16
EXPERT_FF = D_MODEL
CAPACITY_FACTOR = 2
EPS = 1e-6
NEG = -1e30
LANES = 128
VMEM_LIMIT = 56 * 1024 * 1024


def _cparams(sem, vmem=VMEM_LIMIT):
    return pltpu.CompilerParams(dimension_semantics=sem, vmem_limit_bytes=vmem)


def _dot_nt(a, b):
    return lax.dot_general(a, b, (((1,), (1,)), ((), ())), preferred_element_type=F32)


def _in_proj_kernel(x_ref, g_ref, w_ref, gain_ref, flag_ref, o_ref, xn_ref, *, tn):
    x = x_ref[...]
    ms = jnp.mean(x * x, axis=-1, keepdims=True)
    xn_ref[...] = (x * lax.rsqrt(ms + EPS) * g_ref[...]).astype(BF16)
    for jt in range(PROJ_WIDTH // tn):
        acc = jnp.dot(xn_ref[...], w_ref[:, jt * tn:(jt + 1) * tn], preferred_element_type=F32)
        for h in range(tn // HEAD_DIM):
            sl = slice(jt * tn + h * HEAD_DIM, jt * tn + (h + 1) * HEAD_DIM)
            a = acc[:, h * HEAD_DIM:(h + 1) * HEAD_DIM]
            r = lax.rsqrt(jnp.mean(a * a, axis=-1, keepdims=True) + EPS)
            scale = jnp.where(flag_ref[:, sl] > 0.0, r, 1.0) * gain_ref[:, sl]
            o_ref[:, sl] = (a * scale).astype(BF16)


def _in_proj(x2, g_attn, w_perm, gain, flag, tm, tn):
    n = x2.shape[0]
    const = lambda i: (0, 0)
    return pl.pallas_call(
        functools.partial(_in_proj_kernel, tn=tn),
        out_shape=jax.ShapeDtypeStruct((n, PROJ_WIDTH), BF16),
        grid=(n // tm,),
        in_specs=[
            pl.BlockSpec((tm, D_MODEL), lambda i: (i, 0)),
            pl.BlockSpec((1, D_MODEL), const),
            pl.BlockSpec((D_MODEL, PROJ_WIDTH), const, pipeline_mode=pl.Buffered(1)),
            pl.BlockSpec((1, PROJ_WIDTH), const),
            pl.BlockSpec((1, PROJ_WIDTH), const),
        ],
        out_specs=pl.BlockSpec((tm, PROJ_WIDTH), lambda i: (i, 0)),
        scratch_shapes=[pltpu.VMEM((tm, D_MODEL), BF16)],
        compiler_params=_cparams(("parallel",)),
        name="in_proj",
    )(x2, g_attn, w_perm, gain, flag)


def _win_kernel(q_ref, kp_ref, kc_ref, kn_ref, vp_ref, vc_ref, vn_ref, bias_ref, sink_ref, g_ref,
                o_ref, obuf_ref, s_ref):
    nsub = q_ref.shape[1] // BLOCK
    i = pl.program_id(1)
    seq = pl.num_programs(1) * nsub * BLOCK
    lane_pos = lax.broadcasted_iota(I32, (1, 3 * BLOCK), 1) - BLOCK

    def window(prev_ref, cur_ref, next_ref, sb, ks):
        lo, hi = (sb - 1) * BLOCK, (sb + 2) * BLOCK
        parts = []
        if lo < 0:
            parts.append(prev_ref[0, :, ks])
        parts.append(cur_ref[0, max(lo, 0):min(hi, nsub * BLOCK), ks])
        if hi > nsub * BLOCK:
            parts.append(next_ref[0, :, ks])
        return jnp.concatenate(parts, axis=0)

    chains = [(sb, kh) for sb in range(nsub) for kh in range(KVA)]
    for ci, (sb, kh) in enumerate(chains):
        rows = slice(sb * BLOCK, (sb + 1) * BLOCK)
        ks = slice(kh * HEAD_DIM, (kh + 1) * HEAD_DIM)
        qs = jnp.concatenate(
            [q_ref[0, rows, (kh * GA + g) * HEAD_DIM:(kh * GA + g + 1) * HEAD_DIM] for g in range(GA)], axis=0)
        s_ref[ci] = _dot_nt(qs, window(kp_ref, kc_ref, kn_ref, sb, ks))
    for ci, (sb, kh) in enumerate(chains):
        rows = slice(sb * BLOCK, (sb + 1) * BLOCK)
        ks = slice(kh * HEAD_DIM, (kh + 1) * HEAD_DIM)
        s = s_ref[ci] + bias_ref[kh]
        if sb == 0 or sb == nsub - 1:
            kpos = (i * nsub + sb) * BLOCK + lane_pos
            s = s + jnp.where((kpos >= 0) & (kpos < seq), 0.0, NEG).astype(F32)
        sk = sink_ref[kh]
        m = jnp.maximum(jnp.max(s, axis=-1, keepdims=True), sk)
        p = jnp.exp(s - m)
        denom = jnp.sum(p, axis=-1, keepdims=True) + jnp.exp(sk - m)
        vw = window(vp_ref, vc_ref, vn_ref, sb, ks)
        o = jnp.dot(p.astype(BF16), vw, preferred_element_type=F32) / denom
        for g in range(GA):
            hs = slice((kh * GA + g) * HEAD_DIM, (kh * GA + g + 1) * HEAD_DIM)
            obuf_ref[rows, hs] = o[g * BLOCK:(g + 1) * BLOCK, :]
    oall = obuf_ref[...]
    r = lax.rsqrt(jnp.mean(oall * oall, axis=-1, keepdims=True) + EPS)
    o_ref[0] = (oall * r * g_ref[...]).astype(BF16)


def _win_attn(proj3, bias_st, sink_st, on_a, nsub):
    b, s, _ = proj3.shape
    tq = nsub * BLOCK
    nb = s // BLOCK
    kcol = (WIDTH_A + 3 * WIDTH_B) // (KVA * HEAD_DIM)
    kw = KVA * HEAD_DIM

    def edge(col, side):
        if side < 0:
            return pl.BlockSpec((1, BLOCK, kw), lambda bi, i: (bi, jnp.maximum(i * nsub - 1, 0), col))
        return pl.BlockSpec((1, BLOCK, kw), lambda bi, i: (bi, jnp.minimum(i * nsub + nsub, nb - 1), col))

    def main(col):
        return pl.BlockSpec((1, tq, kw), lambda bi, i: (bi, i, col))

    return pl.pallas_call(
        _win_kernel,
        out_shape=jax.ShapeDtypeStruct((b, s, WIDTH_A), BF16),
        grid=(b, s // tq),
        in_specs=[
            pl.BlockSpec((1, tq, WIDTH_A), lambda bi, i: (bi, i, 0)),
            edge(kcol, -1), main(kcol), edge(kcol, 1),
            edge(kcol + 1, -1), main(kcol + 1), edge(kcol + 1, 1),
            pl.BlockSpec((KVA, GA * BLOCK, 3 * BLOCK), lambda bi, i: (0, 0, 0)),
            pl.BlockSpec((KVA, GA * BLOCK, 1), lambda bi, i: (0, 0, 0)),
            pl.BlockSpec((1, WIDTH_A), lambda bi, i: (0, 0)),
        ],
        out_specs=pl.BlockSpec((1, tq, WIDTH_A), lambda bi, i: (bi, i, 0)),
        scratch_shapes=[pltpu.VMEM((tq, WIDTH_A), F32),
                        pltpu.VMEM((nsub * KVA, GA * BLOCK, 3 * BLOCK), F32)],
        compiler_params=_cparams(("parallel", "arbitrary")),
        name="win_attn",
    )(proj3, proj3, proj3, proj3, proj3, proj3, proj3, bias_st, sink_st, on_a)


def _na_kernel(q_ref, k0_ref, k1_ref, k2_ref, v0_ref, v1_ref, v2_ref, bias_ref, g_ref, o_ref, obuf_ref, s_ref):
    tq = q_ref.shape[1]
    for h in range(HB):
        hs = slice(h * HEAD_DIM, (h + 1) * HEAD_DIM)
        qh = q_ref[0, :, hs]
        for kb, kr in enumerate((k0_ref, k1_ref, k2_ref)):
            s_ref[h, :, kb * tq:(kb + 1) * tq] = _dot_nt(qh, kr[0, :, hs])
    for h in range(HB):
        hs = slice(h * HEAD_DIM, (h + 1) * HEAD_DIM)
        s = s_ref[h] + bias_ref[0, h]
        m = jnp.max(s, axis=-1, keepdims=True)
        p = jnp.exp(s - m)
        denom = jnp.sum(p, axis=-1, keepdims=True)
        pb = p.astype(BF16)
        pv = jnp.dot(pb[:, 0:tq], v0_ref[0, :, hs], preferred_element_type=F32)
        pv = pv + jnp.dot(pb[:, tq:2 * tq], v1_ref[0, :, hs], preferred_element_type=F32)
        pv = pv + jnp.dot(pb[:, 2 * tq:3 * tq], v2_ref[0, :, hs], preferred_element_type=F32)
        obuf_ref[:, hs] = pv / denom
    oall = obuf_ref[...]
    r = lax.rsqrt(jnp.mean(oall * oall, axis=-1, keepdims=True) + EPS)
    o_ref[0] = (oall * r * g_ref[...]).astype(BF16)


def _na_attn(proj3, bias_pat, on_b):
    b, s, _ = proj3.shape
    tq = NA_QROWS * GRID_W
    nblk = s // tq

    def kvspec(col, off):
        return pl.BlockSpec((1, tq, WIDTH_B), lambda bi, i: (bi, jnp.clip(i - 1, 0, nblk - 3) + off, col))

    def pat(bi, i):
        return (jnp.where(i == 0, 0, jnp.where(i == nblk - 1, 2, 1)), 0, 0, 0)

    return pl.pallas_call(
        _na_kernel,
        out_shape=jax.ShapeDtypeStruct((b, s, WIDTH_B), BF16),
        grid=(b, nblk),
        in_specs=[
            pl.BlockSpec((1, tq, WIDTH_B), lambda bi, i: (bi, i, 1)),
            kvspec(2, 0), kvspec(2, 1), kvspec(2, 2),
            kvspec(3, 0), kvspec(3, 1), kvspec(3, 2),
            pl.BlockSpec((1, HB, tq, 3 * tq), pat),
            pl.BlockSpec((1, WIDTH_B), lambda bi, i: (0, 0)),
        ],
        out_specs=pl.BlockSpec((1, tq, WIDTH_B), lambda bi, i: (bi, i, 0)),
        scratch_shapes=[pltpu.VMEM((tq, WIDTH_B), F32), pltpu.VMEM((HB, tq, 3 * tq), F32)],
        compiler_params=_cparams(("parallel", "arbitrary")),
        name="na_attn",
    )(proj3, proj3, proj3, proj3, proj3, proj3, proj3, bias_pat, on_b)


def _out_proj_kernel(oa_ref, ob_ref, wo_ref, x_ref, g_ref, wr_ref, x1_ref, hp_ref, lg_ref):
    tm = x_ref.shape[0]
    half = D_MODEL // 2
    tn = 4 * LANES
    ssq = jnp.zeros((tm, 1), F32)
    for jt in range(D_MODEL // tn):
        cs = slice(jt * tn, (jt + 1) * tn)
        acc = jnp.dot(oa_ref[...], wo_ref[0:WIDTH_A, cs], preferred_element_type=F32)
        acc = acc + jnp.dot(ob_ref[...], wo_ref[WIDTH_A:WIDTH_A + WIDTH_B, cs], preferred_element_type=F32)
        x1t = x_ref[:, cs] + acc
        x1_ref[:, cs] = x1t
        ssq = ssq + jnp.sum(x1t * x1t, axis=-1, keepdims=True)
    h32 = x1_ref[...] * lax.rsqrt(ssq * (1.0 / D_MODEL) + EPS) * g_ref[...]
    h_hi = h32.astype(BF16)
    h_hi32 = h_hi.astype(F32)
    bits = pltpu.bitcast(h_hi32, U32)
    words = (bits[:, 0:half] >> 16) | (bits[:, half:D_MODEL] & jnp.uint32(0xFFFF0000))
    nwords = half // LANES
    for q in range(nwords):
        hp_ref[pl.ds(q, tm, stride=nwords), :] = words[:, q * LANES:(q + 1) * LANES]
    h_lo = (h32 - h_hi32).astype(BF16)
    r = jnp.dot(jnp.concatenate([h_hi, h_lo], axis=0), wr_ref[...], preferred_element_type=F32)
    ne = N_EXPERTS
    lg_ref[...] = (r[0:tm, 0:ne] + r[0:tm, ne:2 * ne]) + (r[tm:2 * tm, 0:ne] + r[tm:2 * tm, ne:2 * ne])


def _out_proj(oa, ob, wo, x2, g_ffn, wr2, tm):
    n = x2.shape[0]
    nwords = D_MODEL // 2 // LANES
    return pl.pallas_call(
        _out_proj_kernel,
        out_shape=(jax.ShapeDtypeStruct((n, D_MODEL), F32),
                   jax.ShapeDtypeStruct((n * nwords, LANES), U32),
                   jax.ShapeDtypeStruct((n, N_EXPERTS), F32)),
        grid=(n // tm,),
        in_specs=[
            pl.BlockSpec((tm, WIDTH_A), lambda i: (i, 0)),
            pl.BlockSpec((tm, WIDTH_B), lambda i: (i, 0)),
            pl.BlockSpec((WIDTH_A + WIDTH_B, D_MODEL), lambda i: (0, 0), pipeline_mode=pl.Buffered(1)),
            pl.BlockSpec((tm, D_MODEL), lambda i: (i, 0)),
            pl.BlockSpec((1, D_MODEL), lambda i: (0, 0)),
            pl.BlockSpec((D_MODEL, 2 * N_EXPERTS), lambda i: (0, 0)),
        ],
        out_specs=(pl.BlockSpec((tm, D_MODEL), lambda i: (i, 0)),
                   pl.BlockSpec((tm * nwords, LANES), lambda i: (i, 0)),
                   pl.BlockSpec((tm, N_EXPERTS), lambda i: (i, 0))),
        compiler_params=_cparams(("parallel",)),
        name="out_proj",
    )(oa, ob, wo, x2, g_ffn, wr2)


def _route_kernel(lg_ref, idx_ref, gate_ref, aff_ref, *, cap):
    ne, rr, _ = lg_ref.shape
    lg = lg_ref[...]
    ex = jnp.exp(lg - jnp.max(lg, axis=0, keepdims=True))
    aff_ref[...] = ex / jnp.sum(ex, axis=0, keepdims=True)

    upper = (lax.broadcasted_iota(I32, (LANES, LANES), 0) <= lax.broadcasted_iota(I32, (LANES, LANES), 1)).astype(BF16)
    lower = (lax.broadcasted_iota(I32, (rr, rr), 1) < lax.broadcasted_iota(I32, (rr, rr), 0)).astype(BF16)
    pcol = lax.broadcasted_iota(I32, (1, cap), 1).astype(F32)
    rowid = lax.broadcasted_iota(I32, (rr, 1), 0).astype(F32)
    laneid = lax.broadcasted_iota(I32, (LANES, 1), 0).astype(F32)

    def total(v):
        return jnp.sum(jnp.sum(v, axis=1, keepdims=True), axis=0, keepdims=True)

    def prefix(mask):
        within = jnp.dot(mask.astype(BF16), upper, preferred_element_type=F32)
        tot = within[:, LANES - 1:LANES]
        off = jnp.dot(lower, jnp.broadcast_to(tot, (rr, LANES)).astype(BF16), preferred_element_type=F32)
        return within + off, off[:, 0:1], tot

    def gather_rows(vals_t, onehot_t):
        return jnp.dot(vals_t.astype(BF16), onehot_t, preferred_element_type=F32)

    def body(e, carry):
        a = aff_ref[e]
        bits = pltpu.bitcast(a, I32)
        thr = jnp.zeros((1, 1), I32)
        for bit in range(30, -1, -1):
            cand = thr | jnp.int32(1 << bit)
            cnt = total((bits >= cand).astype(F32))
            thr = jnp.where(cnt >= cap, cand, thr)
        gt = bits > thr
        eq = bits == thr
        need = cap - total(gt.astype(F32))
        eq_rank, _, _ = prefix(eq.astype(F32))
        sel = gt | (eq & (eq_rank <= need))
        pos, excl, tot = prefix(sel.astype(F32))
        onehot_t = ((excl <= pcol) & (pcol < excl + tot))
        oh = onehot_t.astype(BF16)
        row_of = jnp.sum(jnp.where(onehot_t, rowid, 0.0), axis=0, keepdims=True)
        pos_t = pos.T
        pos_hi = jnp.floor(pos_t * (1.0 / 64.0))
        pos_lo = pos_t - 64.0 * pos_hi
        prow = 64.0 * gather_rows(pos_hi, oh) + gather_rows(pos_lo, oh)
        local = jnp.sum((prow <= pcol).astype(F32), axis=0, keepdims=True)
        idx_ref[e] = (row_of * float(LANES) + local).astype(I32)
        a_t = a.T
        a1 = a_t.astype(BF16).astype(F32)
        a2 = (a_t - a1).astype(BF16).astype(F32)
        a3 = a_t - a1 - a2
        arow = gather_rows(a1, oh) + gather_rows(a2, oh) + gather_rows(a3, oh)
        gate_ref[e] = jnp.sum(jnp.where(laneid == local, arow, 0.0), axis=0, keepdims=True)
        return carry

    lax.fori_loop(0, ne, body, 0)


def _route(lg3, cap):
    ne, rr, _ = lg3.shape
    return pl.pallas_call(
        functools.partial(_route_kernel, cap=cap),
        out_shape=(jax.ShapeDtypeStruct((ne, 1, cap), I32), jax.ShapeDtypeStruct((ne, 1, cap), F32)),
        scratch_shapes=[pltpu.VMEM((ne, rr, LANES), F32)],
        compiler_params=pltpu.CompilerParams(vmem_limit_bytes=VMEM_LIMIT),
        name="route",
    )(lg3)


def _ffn_kernel(idxp_ref, idx_ref, idxn_ref, gate_ref, hp_hbm, y_in_hbm, wg_ref, wu_ref, wd_ref, y_hbm,
                gbuf, xbuf, acc, ybuf, sem_h, sem_yi, sem_yo):
    del y_in_hbm
    tm = xbuf.shape[0]
    half = D_MODEL // 2
    nwords = half // LANES
    e = pl.program_id(0)
    c = pl.program_id(2)
    nj = pl.num_programs(1)
    nc = pl.num_programs(2)
    step = e * nj + pl.program_id(1)
    last = pl.num_programs(0) * nj - 1
    slot = lax.rem(step, 2)

    def h_copy(ids, r, sl):
        return pltpu.make_async_copy(hp_hbm.at[ids[0, 0, r]], gbuf.at[sl, pl.ds(r * nwords, nwords), :], sem_h.at[sl])

    def yi_copy(ids, r):
        return pltpu.make_async_copy(y_hbm.at[pl.ds(ids[0, 0, r], 1), :], ybuf.at[pl.ds(r, 1), :], sem_yi)

    def yo_copy(ids, r):
        return pltpu.make_async_copy(ybuf.at[pl.ds(r, 1), :], y_hbm.at[pl.ds(ids[0, 0, r], 1), :], sem_yo)

    def for_rows(fn):
        def body(r, carry):
            fn(r)
            return carry
        lax.fori_loop(0, tm, body, 0, unroll=8)

    def start_rows(fn):
        for r in range(tm):
            fn(r).start()

    @pl.when(c == 0)
    def _():
        @pl.when(step == 0)
        def _():
            for_rows(lambda r: h_copy(idx_ref, r, slot).start())

        for_rows(lambda r: h_copy(idx_ref, r, slot).wait())
        for q in range(nwords):
            w = gbuf[slot, pl.ds(q, tm, stride=nwords), :]
            xbuf[:, q * LANES:(q + 1) * LANES] = pltpu.bitcast(w << 16, F32).astype(BF16)
            xbuf[:, half + q * LANES:half + (q + 1) * LANES] = (
                pltpu.bitcast(w & jnp.uint32(0xFFFF0000), F32).astype(BF16))
        acc[...] = jnp.zeros_like(acc)

    @pl.when(c == 1)
    def _():
        @pl.when(step > 0)
        def _():
            for_rows(lambda r: yo_copy(idxp_ref, r).wait())

        start_rows(lambda r: yi_copy(idx_ref, r))

        @pl.when(step < last)
        def _():
            start_rows(lambda r: h_copy(idxn_ref, r, 1 - slot))

    x = xbuf[...]
    g = jnp.dot(x, wg_ref[0], preferred_element_type=F32)
    u = jnp.dot(x, wu_ref[0], preferred_element_type=F32)
    hid = (g * jax.nn.sigmoid(g)) * u
    acc[...] += jnp.dot(hid.astype(BF16), wd_ref[0], preferred_element_type=F32)

    @pl.when(c == nc - 1)
    def _():
        for_rows(lambda r: yi_copy(idx_ref, r).wait())
        onehot = lax.broadcasted_iota(I32, (1, N_EXPERTS), 1) == e
        gcol = jnp.sum(jnp.where(onehot, gate_ref[...], 0.0), axis=1, keepdims=True)
        ybuf[...] = ybuf[...] + acc[...] * gcol
        start_rows(lambda r: yo_copy(idx_ref, r))

        @pl.when(step == last)
        def _():
            for_rows(lambda r: yo_copy(idx_ref, r).wait())


def _ffn(idx3, gate_t, hp3, x1, wg, wu, wd, tm, tf):
    ne = N_EXPERTS
    cap = gate_t.shape[0]
    nj = cap // tm
    nc = EXPERT_FF // tf
    assert nc >= 3, "the row-traffic schedule uses ff chunks 0, 1 and the last one"
    n = x1.shape[0]
    nsteps = ne * nj
    nwords = D_MODEL // 2 // LANES

    def ids(off):
        return pl.BlockSpec((1, 1, tm), lambda e, j, c: (jnp.clip(e * nj + j + off, 0, nsteps - 1), 0, 0),
                            memory_space=pltpu.SMEM)

    return pl.pallas_call(
        _ffn_kernel,
        out_shape=jax.ShapeDtypeStruct((n, D_MODEL), F32),
        grid=(ne, nj, nc),
        in_specs=[
            ids(-1), ids(0), ids(1),
            pl.BlockSpec((tm, ne), lambda e, j, c: (j, 0)),
            pl.BlockSpec(memory_space=pl.ANY),
            pl.BlockSpec(memory_space=pl.ANY),
            pl.BlockSpec((1, D_MODEL, tf), lambda e, j, c: (e, 0, c)),
            pl.BlockSpec((1, D_MODEL, tf), lambda e, j, c: (e, 0, c)),
            pl.BlockSpec((1, tf, D_MODEL), lambda e, j, c: (e, c, 0)),
        ],
        out_specs=pl.BlockSpec(memory_space=pl.ANY),
        scratch_shapes=[
            pltpu.VMEM((2, tm * nwords, LANES), U32),
            pltpu.VMEM((tm, D_MODEL), BF16),
            pltpu.VMEM((tm, D_MODEL), F32),
            pltpu.VMEM((tm, D_MODEL), F32),
            pltpu.SemaphoreType.DMA((2,)),
            pltpu.SemaphoreType.DMA(()),
            pltpu.SemaphoreType.DMA(()),
        ],
        input_output_aliases={5: 0},
        compiler_params=_cparams(("arbitrary", "arbitrary", "arbitrary")),
        name="ffn",
    )(idx3, idx3, idx3, gate_t, hp3, x1, wg, wu, wd)


def _t5_bucket(rel):
    nb = N_BUCKETS // 2
    max_exact = nb // 2
    ret = np.where(rel > 0, nb, 0)
    n = np.abs(rel)
    nf = np.maximum(n, 1).astype(np.float32)
    large = max_exact + (np.log(nf / max_exact) / np.log(MAX_DISTANCE / max_exact) * (nb - max_exact)).astype(np.int32)
    large = np.minimum(large, nb - 1)
    return (ret + np.where(n < max_exact, n, large)).astype(np.int32)


def _win_bias(t5_table):
    rel = np.arange(-WINDOW, WINDOW + 1)
    vals = t5_table[_t5_bucket(rel)].astype(F32).T
    pad = jnp.full((HA, BLOCK - 1), NEG, F32)
    u = jnp.concatenate([pad, vals, pad], axis=1)
    return jnp.stack([u[:, BLOCK - 1 - il:BLOCK - 1 - il + 3 * BLOCK] for il in range(BLOCK)], axis=1)


def _na_bias(rpb, rows):
    ndr, ndc = 2 * NA_ROWS - 1, 2 * NA_COLS - 1
    pad = jnp.full((HB, ndr, GRID_W - NA_COLS), NEG, F32)
    u = jnp.concatenate([pad, rpb.astype(F32), pad], axis=2)
    col = jnp.stack([u[:, :, GRID_W - 1 - qc:2 * GRID_W - 1 - qc] for qc in range(GRID_W)], axis=2)
    qc = np.arange(GRID_W)[:, None]
    kc = np.arange(GRID_W)[None, :]
    cs = np.clip(qc - NA_COLS // 2, 0, GRID_W - NA_COLS)
    col = jnp.where((kc >= cs) & (kc < cs + NA_COLS), col, NEG)
    masked = jnp.full((HB, GRID_W, GRID_W), NEG, F32)
    nblk = rows // NA_QROWS
    pats = []
    for blk in (0, 1, nblk - 1):
        r0 = blk * NA_QROWS
        ws = int(np.clip(r0 - NA_ROWS // 2, 0, rows - NA_KROWS))
        qrows = []
        for qi in range(NA_QROWS):
            qrow = r0 + qi
            rs = int(np.clip(qrow - NA_ROWS // 2, 0, rows - NA_ROWS))
            blocks = []
            for kj in range(NA_KROWS):
                krow = ws + kj
                blocks.append(col[:, krow - qrow + NA_ROWS - 1] if rs <= krow < rs + NA_ROWS else masked)
            qrows.append(jnp.concatenate(blocks, axis=-1))
        pats.append(jnp.concatenate(qrows, axis=-2))
    return jnp.stack(pats, axis=0)


def _prepare(t5_table, norm_attn, w_in, q_norm_a, k_norm_a, q_norm_b, k_norm_b, sink_a, rpb_b,
             out_norm_a, out_norm_b, w_out, norm_ffn, w_router, w_gate, w_up, w_down):
    qa, ka, va = WIDTH_A, KVA * HEAD_DIM, KVA * HEAD_DIM
    o_ka, o_va, o_qb = qa, qa + ka, qa + ka + va
    o_kb, o_vb = o_qb + WIDTH_B, o_qb + 2 * WIDTH_B
    w_perm = jnp.concatenate([w_in[:, 0:qa], w_in[:, o_qb:o_kb], w_in[:, o_kb:o_vb], w_in[:, o_vb:],
                              w_in[:, o_ka:o_va], w_in[:, o_va:o_qb]], axis=1).astype(BF16)
    scale = HEAD_DIM ** -0.5
    ones = lambda k: jnp.ones((k,), F32)
    zeros = lambda k: jnp.zeros((k,), F32)
    gain = jnp.concatenate([jnp.tile(q_norm_a, HA) * scale, jnp.tile(q_norm_b, HB) * scale, jnp.tile(k_norm_b, HB),
                            ones(WIDTH_B), jnp.tile(k_norm_a, KVA), ones(va)])[None, :]
    flag = jnp.concatenate([ones(qa), ones(WIDTH_B), ones(WIDTH_B), zeros(WIDTH_B), ones(ka), zeros(va)])[None, :]

    bias_st = _win_bias(t5_table).reshape(KVA, GA * BLOCK, 3 * BLOCK)
    sink_st = jnp.repeat(sink_a.astype(F32), BLOCK).reshape(KVA, GA * BLOCK, 1)

    wr_hi = w_router.astype(BF16)
    wr_lo = (w_router - wr_hi.astype(F32)).astype(BF16)
    return dict(
        w_perm=w_perm, gain=gain, flag=flag, g_attn=norm_attn[None, :], bias_st=bias_st, sink_st=sink_st,
        rpb=rpb_b, on_a=out_norm_a[None, :], on_b=out_norm_b[None, :],
        wo=w_out.astype(BF16), g_ffn=norm_ffn[None, :], wr2=jnp.concatenate([wr_hi, wr_lo], axis=1),
        wg=w_gate.astype(BF16), wu=w_up.astype(BF16), wd=w_down.astype(BF16))


def _pick(n, pref):
    t = pref
    while n % t:
        t //= 2
    return t


def _trunk(x, p):
    b, s, _ = x.shape
    n = b * s
    x2 = x.reshape(n, D_MODEL)
    proj = _in_proj(x2, p["g_attn"], p["w_perm"], p["gain"], p["flag"], _pick(n, 512), 512)
    proj3 = proj.reshape(b, s, PROJ_WIDTH)
    oa = _win_attn(proj3, p["bias_st"], p["sink_st"], p["on_a"], _pick(s // BLOCK, 4))
    ob = _na_attn(proj3, _na_bias(p["rpb"], s // GRID_W), p["on_b"])
    x1, hp, logits = _out_proj(oa.reshape(n, WIDTH_A), ob.reshape(n, WIDTH_B), p["wo"], x2, p["g_ffn"], p["wr2"],
                               _pick(n, 512))
    cap =CAPACITY_FACTOR * n // N_EXPERTS
    idx, gate = _route(logits.T.reshape(N_EXPERTS, n // LANES, LANES), cap)
    tm = _pick(cap, 1024)
    hp3 = hp.reshape(n, D_MODEL // 2 // LANES, LANES)
    y = _ffn(idx.reshape(N_EXPERTS * (cap // tm), 1, tm), gate.reshape(N_EXPERTS, cap).T, hp3, x1,
             p["wg"], p["wu"], p["wd"], tm, 512)
    return y.reshape(b, s, D_MODEL)


def kernel(x_prompt, x_sample, t5_table, norm_attn, w_in, q_norm_a, k_norm_a, q_norm_b, k_norm_b, sink_a, rpb_b,
           out_norm_a, out_norm_b, w_out, norm_ffn, w_router, w_gate, w_up, w_down):
    p = _prepare(t5_table, norm_attn[0], w_in[0], q_norm_a[0], k_norm_a[0], q_norm_b[0], k_norm_b[0], sink_a[0],
                 rpb_b[0], out_norm_a[0], out_norm_b[0], w_out[0], norm_ffn[0], w_router[0], w_gate[0], w_up[0],
                 w_down[0])
    return (_trunk(x_prompt, p), _trunk(x_sample, p))
```

```python
import functools

import numpy as np
import jax
import jax.numpy as jnp
from jax import lax
from jax.experimental import pallas as pl
from jax.experimental.pallas import tpu as pltpu

F32 = jnp.float32
BF16 = jnp.bfloat16
I32 = jnp.int32
U32 = jnp.uint32

D_MODEL = 2048
HEAD_DIM = 128
HA = 8
KVA = 2
GA = HA // KVA
HB = 8
WIDTH_A = HA * HEAD_DIM
WIDTH_B = HB * HEAD_DIM
PROJ_WIDTH = WIDTH_A + 2 * KVA * HEAD_DIM + 3 * WIDTH_B
WINDOW = 128
BLOCK = 128
N_BUCKETS = 32
MAX_DISTANCE = 128
GRID_W = 64
NA_ROWS = 8
NA_COLS = 16
NA_QROWS = 4
NA_KROWS = 3 * NA_QROWS
N_EXPERTS = 16
EXPERT_FF = D_MODEL
CAPACITY_FACTOR = 2
EPS = 1e-6
NEG = -1e30
LOG2E = 1.4426950408889634
LANES = 128
VMEM_LIMIT = 56 * 1024 * 1024


def _cparams(sem, vmem=VMEM_LIMIT):
    return pltpu.CompilerParams(dimension_semantics=sem, vmem_limit_bytes=vmem)


def _dot_nt(a, b):
    return lax.dot_general(a, b, (((1,), (1,)), ((), ())), preferred_element_type=F32)


def _in_proj_kernel(x_ref, g_ref, w_ref, gain_ref, flag_ref, o_ref, xn_ref, *, tn):
    x = x_ref[...]
    ms = jnp.mean(x * x, axis=-1, keepdims=True)
    xn_ref[...] = (x * lax.rsqrt(ms + EPS) * g_ref[...]).astype(BF16)
    for jt in range(PROJ_WIDTH // tn):
        acc = jnp.dot(xn_ref[...], w_ref[:, jt * tn:(jt + 1) * tn], preferred_element_type=F32)
        for h in range(tn // HEAD_DIM):
            sl = slice(jt * tn + h * HEAD_DIM, jt * tn + (h + 1) * HEAD_DIM)
            a = acc[:, h * HEAD_DIM:(h + 1) * HEAD_DIM]
            r = lax.rsqrt(jnp.mean(a * a, axis=-1, keepdims=True) + EPS)
            scale = jnp.where(flag_ref[:, sl] > 0.0, r, 1.0) * gain_ref[:, sl]
            o_ref[:, sl] = (a * scale).astype(BF16)


def _in_proj(x2, g_attn, w_perm, gain, flag, tm, tn):
    n = x2.shape[0]
    const = lambda i: (0, 0)
    return pl.pallas_call(
        functools.partial(_in_proj_kernel, tn=tn),
        out_shape=jax.ShapeDtypeStruct((n, PROJ_WIDTH), BF16),
        grid=(n // tm,),
        in_specs=[
            pl.BlockSpec((tm, D_MODEL), lambda i: (i, 0)),
            pl.BlockSpec((1, D_MODEL), const),
            pl.BlockSpec((D_MODEL, PROJ_WIDTH), const, pipeline_mode=pl.Buffered(1)),
            pl.BlockSpec((1, PROJ_WIDTH), const),
            pl.BlockSpec((1, PROJ_WIDTH), const),
        ],
        out_specs=pl.BlockSpec((tm, PROJ_WIDTH), lambda i: (i, 0)),
        scratch_shapes=[pltpu.VMEM((tm, D_MODEL), BF16)],
        compiler_params=_cparams(("parallel",)),
        name="in_proj",
    )(x2, g_attn, w_perm, gain, flag)


def _win_kernel(q_ref, kp_ref, kc_ref, kn_ref, vp_ref, vc_ref, vn_ref, bias_ref, sink_ref, g_ref,
                o_ref, obuf_ref, s_ref):
    nsub = q_ref.shape[1] // BLOCK
    i = pl.program_id(1)
    seq = pl.num_programs(1) * nsub * BLOCK
    lane_pos = lax.broadcasted_iota(I32, (1, 3 * BLOCK), 1) - BLOCK

    def window(prev_ref, cur_ref, next_ref, sb, ks):
        lo, hi = (sb - 1) * BLOCK, (sb + 2) * BLOCK
        parts = []
        if lo < 0:
            parts.append(prev_ref[0, :, ks])
        parts.append(cur_ref[0, max(lo, 0):min(hi, nsub * BLOCK), ks])
        if hi > nsub * BLOCK:
            parts.append(next_ref[0, :, ks])
        return jnp.concatenate(parts, axis=0)

    chains = [(sb, kh) for sb in range(nsub) for kh in range(KVA)]
    for ci, (sb, kh) in enumerate(chains):
        rows = slice(sb * BLOCK, (sb + 1) * BLOCK)
        ks = slice(kh * HEAD_DIM, (kh + 1) * HEAD_DIM)
        qs = jnp.concatenate(
            [q_ref[0, rows, (kh * GA + g) * HEAD_DIM:(kh * GA + g + 1) * HEAD_DIM] for g in range(GA)], axis=0)
        s_ref[ci] = _dot_nt(qs, window(kp_ref, kc_ref, kn_ref, sb, ks))
    for ci, (sb, kh) in enumerate(chains):
        rows = slice(sb * BLOCK, (sb + 1) * BLOCK)
        ks = slice(kh * HEAD_DIM, (kh + 1) * HEAD_DIM)
        s = s_ref[ci] + bias_ref[kh]
        if sb == 0 or sb == nsub - 1:
            kpos = (i * nsub + sb) * BLOCK + lane_pos
            s = s + jnp.where((kpos >= 0) & (kpos < seq), 0.0, NEG).astype(F32)
        sk = sink_ref[kh]
        m = jnp.maximum(jnp.max(s, axis=-1, keepdims=True), sk)
        p = jnp.exp2(s - m)
        denom = jnp.sum(p, axis=-1, keepdims=True) + jnp.exp2(sk - m)
        vw = window(vp_ref, vc_ref, vn_ref, sb, ks)
        o = jnp.dot(p.astype(BF16), vw, preferred_element_type=F32) / denom
        for g in range(GA):
            hs = slice((kh * GA + g) * HEAD_DIM, (kh * GA + g + 1) * HEAD_DIM)
            obuf_ref[rows, hs] = o[g * BLOCK:(g + 1) * BLOCK, :]
    oall = obuf_ref[...]
    r = lax.rsqrt(jnp.mean(oall * oall, axis=-1, keepdims=True) + EPS)
    o_ref[0] = (oall * r * g_ref[...]).astype(BF16)


def _win_attn(proj3, bias_st, sink_st, on_a, nsub):
    b, s, _ = proj3.shape
    tq = nsub * BLOCK
    nb = s // BLOCK
    kcol = (WIDTH_A + 3 * WIDTH_B) // (KVA * HEAD_DIM)
    kw = KVA * HEAD_DIM

    def edge(col, side):
        if side < 0:
            return pl.BlockSpec((1, BLOCK, kw), lambda bi, i: (bi, jnp.maximum(i * nsub - 1, 0), col))
        return pl.BlockSpec((1, BLOCK, kw), lambda bi, i: (bi, jnp.minimum(i * nsub + nsub, nb - 1), col))

    def main(col):
        return pl.BlockSpec((1, tq, kw), lambda bi, i: (bi, i, col))

    return pl.pallas_call(
        _win_kernel,
        out_shape=jax.ShapeDtypeStruct((b, s, WIDTH_A), BF16),
        grid=(b, s // tq),
        in_specs=[
            pl.BlockSpec((1, tq, WIDTH_A), lambda bi, i: (bi, i, 0)),
            edge(kcol, -1), main(kcol), edge(kcol, 1),
            edge(kcol + 1, -1), main(kcol + 1), edge(kcol + 1, 1),
            pl.BlockSpec((KVA, GA * BLOCK, 3 * BLOCK), lambda bi, i: (0, 0, 0)),
            pl.BlockSpec((KVA, GA * BLOCK, 1), lambda bi, i: (0, 0, 0)),
            pl.BlockSpec((1, WIDTH_A), lambda bi, i: (0, 0)),
        ],
        out_specs=pl.BlockSpec((1, tq, WIDTH_A), lambda bi, i: (bi, i, 0)),
        scratch_shapes=[pltpu.VMEM((tq, WIDTH_A), F32),
                        pltpu.VMEM((nsub * KVA, GA * BLOCK, 3 * BLOCK), F32)],
        compiler_params=_cparams(("parallel", "arbitrary")),
        name="win_attn",
    )(proj3, proj3, proj3, proj3, proj3, proj3, proj3, bias_st, sink_st, on_a)


def _na_kernel(q_ref, k0_ref, k1_ref, k2_ref, v0_ref, v1_ref, v2_ref, bias_ref, g_ref, o_ref, obuf_ref, s_ref):
    tq = q_ref.shape[1]
    for h in range(HB):
        hs = slice(h * HEAD_DIM, (h + 1) * HEAD_DIM)
        qh = q_ref[0, :, hs]
        for kb, kr in enumerate((k0_ref, k1_ref, k2_ref)):
            s_ref[h, :, kb * tq:(kb + 1) * tq] = _dot_nt(qh, kr[0, :, hs])
    for h in range(HB):
        hs = slice(h * HEAD_DIM, (h + 1) * HEAD_DIM)
        s = s_ref[h] + bias_ref[0, h]
        m = jnp.max(s, axis=-1, keepdims=True)
        p = jnp.exp2(s - m)
        denom = jnp.sum(p, axis=-1, keepdims=True)
        pb = p.astype(BF16)
        pv = jnp.dot(pb[:, 0:tq], v0_ref[0, :, hs], preferred_element_type=F32)
        pv = pv + jnp.dot(pb[:, tq:2 * tq], v1_ref[0, :, hs], preferred_element_type=F32)
        pv = pv + jnp.dot(pb[:, 2 * tq:3 * tq], v2_ref[0, :, hs], preferred_element_type=F32)
        obuf_ref[:, hs] = pv / denom
    oall = obuf_ref[...]
    r = lax.rsqrt(jnp.mean(oall * oall, axis=-1, keepdims=True) + EPS)
    o_ref[0] = (oall * r * g_ref[...]).astype(BF16)


def _na_attn(proj3, bias_pat, on_b):
    b, s, _ = proj3.shape
    tq = NA_QROWS * GRID_W
    nblk = s // tq
    assert nblk >= 4, "row-block patterns (top / interior / bottom) assume at least four row blocks"

    def kvspec(col, off):
        return pl.BlockSpec((1, tq, WIDTH_B), lambda bi, i: (bi, jnp.clip(i - 1, 0, nblk - 3) + off, col))

    def pat(bi, i):
        return (jnp.where(i == 0, 0, jnp.where(i == nblk - 1, 2, 1)), 0, 0, 0)

    return pl.pallas_call(
        _na_kernel,
        out_shape=jax.ShapeDtypeStruct((b, s, WIDTH_B), BF16),
        grid=(b, nblk),
        in_specs=[
            pl.BlockSpec((1, tq, WIDTH_B), lambda bi, i: (bi, i, 1)),
            kvspec(2, 0), kvspec(2, 1), kvspec(2, 2),
            kvspec(3, 0), kvspec(3, 1), kvspec(3, 2),
            pl.BlockSpec((1, HB, tq, 3 * tq), pat),
            pl.BlockSpec((1, WIDTH_B), lambda bi, i: (0, 0)),
        ],
        out_specs=pl.BlockSpec((1, tq, WIDTH_B), lambda bi, i: (bi, i, 0)),
        scratch_shapes=[pltpu.VMEM((tq, WIDTH_B), F32), pltpu.VMEM((HB, tq, 3 * tq), F32)],
        compiler_params=_cparams(("parallel", "arbitrary")),
        name="na_attn",
    )(proj3, proj3, proj3, proj3, proj3, proj3, proj3, bias_pat, on_b)


def _out_proj_kernel(oa_ref, ob_ref, wo_ref, x_ref, g_ref, wr_ref, x1_ref, hp_ref, lg_ref):
    tm = x_ref.shape[0]
    half = D_MODEL // 2
    nwords = half // LANES
    tn = 4 * LANES
    ne = N_EXPERTS
    ssq = jnp.zeros((tm, 1), F32)
    for jt in range(D_MODEL // tn):
        cs = slice(jt * tn, (jt + 1) * tn)
        acc = jnp.dot(oa_ref[...], wo_ref[0:WIDTH_A, cs], preferred_element_type=F32)
        acc = acc + jnp.dot(ob_ref[...], wo_ref[WIDTH_A:WIDTH_A + WIDTH_B, cs], preferred_element_type=F32)
        x1t = x_ref[:, cs] + acc
        x1_ref[:, cs] = x1t
        ssq = ssq + jnp.sum(x1t * x1t, axis=-1, keepdims=True)
    h32 = x1_ref[...] * lax.rsqrt(ssq * (1.0 / D_MODEL) + EPS) * g_ref[...]
    h_hi = h32.astype(BF16)
    h_hi32 = h_hi.astype(F32)
    bits = pltpu.bitcast(h_hi32, U32)
    words = (bits[:, 0:half] >> 16) | (bits[:, half:D_MODEL] & jnp.uint32(0xFFFF0000))
    for q in range(nwords):
        hp_ref[pl.ds(q, tm, stride=nwords), :] = words[:, q * LANES:(q + 1) * LANES]
    h_lo = (h32 - h_hi32).astype(BF16)
    r = jnp.dot(jnp.concatenate([h_hi, h_lo], axis=0), wr_ref[...], preferred_element_type=F32)
    lg_ref[...] = (r[0:tm, 0:ne] + r[0:tm, ne:2 * ne]) + (r[tm:2 * tm, 0:ne] + r[tm:2 * tm, ne:2 * ne])


def _out_proj(oa, ob, wo, x2, g_ffn, wr2, tm):
    n = x2.shape[0]
    nwords = D_MODEL // 2 // LANES
    return pl.pallas_call(
        _out_proj_kernel,
        out_shape=(jax.ShapeDtypeStruct((n, D_MODEL), F32),
                   jax.ShapeDtypeStruct((n * nwords, LANES), U32),
                   jax.ShapeDtypeStruct((n, N_EXPERTS), F32)),
        grid=(n // tm,),
        in_specs=[
            pl.BlockSpec((tm, WIDTH_A), lambda i: (i, 0)),
            pl.BlockSpec((tm, WIDTH_B), lambda i: (i, 0)),
            pl.BlockSpec((WIDTH_A + WIDTH_B, D_MODEL), lambda i: (0, 0), pipeline_mode=pl.Buffered(1)),
            pl.BlockSpec((tm, D_MODEL), lambda i: (i, 0)),
            pl.BlockSpec((1, D_MODEL), lambda i: (0, 0)),
            pl.BlockSpec((D_MODEL, 2 * N_EXPERTS), lambda i: (0, 0)),
        ],
        out_specs=(pl.BlockSpec((tm, D_MODEL), lambda i: (i, 0)),
                   pl.BlockSpec((tm * nwords, LANES), lambda i: (i, 0)),
                   pl.BlockSpec((tm, N_EXPERTS), lambda i: (i, 0))),
        compiler_params=_cparams(("parallel",)),
        name="out_proj",
    )(oa, ob, wo, x2, g_ffn, wr2)


def _route_kernel(lg_ref, idx_ref, gate_ref, aff_ref, *, cap):
    ne, rr, _ = lg_ref.shape
    lg = lg_ref[...]
    ex = jnp.exp(lg - jnp.max(lg, axis=0, keepdims=True))
    aff_ref[...] = ex / jnp.sum(ex, axis=0, keepdims=True)

    upper = (lax.broadcasted_iota(I32, (LANES, LANES), 0) <= lax.broadcasted_iota(I32, (LANES, LANES), 1)).astype(BF16)
    lower = (lax.broadcasted_iota(I32, (rr, rr), 1) < lax.broadcasted_iota(I32, (rr, rr), 0)).astype(BF16)
    pcol = lax.broadcasted_iota(I32, (1, cap), 1).astype(F32)
    rowid = lax.broadcasted_iota(I32, (rr, 1), 0).astype(F32)
    laneid = lax.broadcasted_iota(I32, (LANES, 1), 0).astype(F32)

    def total(v):
        return jnp.sum(jnp.sum(v, axis=1, keepdims=True), axis=0, keepdims=True)

    def prefix(mask):
        within = jnp.dot(mask.astype(BF16), upper, preferred_element_type=F32)
        tot = within[:, LANES - 1:LANES]
        off = jnp.dot(lower, jnp.broadcast_to(tot, (rr, LANES)).astype(BF16), preferred_element_type=F32)
        return within + off, off[:, 0:1], tot

    def gather_rows(vals_t, onehot_t):
        return jnp.dot(vals_t.astype(BF16), onehot_t, preferred_element_type=F32)

    def body(e, carry):
        a = aff_ref[e]
        bits = pltpu.bitcast(a, I32)
        thr = jnp.zeros((1, 1), I32)
        for bit in range(30, -1, -1):
            cand = thr | jnp.int32(1 << bit)
            cnt = total((bits >= cand).astype(F32))
            thr = jnp.where(cnt >= cap, cand, thr)
        gt = bits > thr
        eq = bits == thr
        need = cap - total(gt.astype(F32))
        eq_rank, _, _ = prefix(eq.astype(F32))
        sel = gt | (eq & (eq_rank <= need))
        pos, excl, tot = prefix(sel.astype(F32))
        onehot_t = ((excl <= pcol) & (pcol < excl + tot))
        oh = onehot_t.astype(BF16)
        row_of = jnp.sum(jnp.where(onehot_t, rowid, 0.0), axis=0, keepdims=True)
        pos_t = pos.T
        pos_hi = jnp.floor(pos_t * (1.0 / 64.0))
        pos_lo = pos_t - 64.0 * pos_hi
        prow = 64.0 * gather_rows(pos_hi, oh) + gather_rows(pos_lo, oh)
        local = jnp.sum((prow <= pcol).astype(F32), axis=0, keepdims=True)
        idx_ref[e] = (row_of * float(LANES) + local).astype(I32)
        a_t = a.T
        a1 = a_t.astype(BF16).astype(F32)
        a2 = (a_t - a1).astype(BF16).astype(F32)
        a3 = a_t - a1 - a2
        arow = gather_rows(a1, oh) + gather_rows(a2, oh) + gather_rows(a3, oh)
        gate_ref[e] = jnp.sum(jnp.where(laneid == local, arow, 0.0), axis=0, keepdims=True)
        return carry

    lax.fori_loop(0, ne, body, 0)


def _route(lg3, cap):
    ne, rr, _ = lg3.shape
    return pl.pallas_call(
        functools.partial(_route_kernel, cap=cap),
        out_shape=(jax.ShapeDtypeStruct((ne, 1, cap), I32), jax.ShapeDtypeStruct((ne, 1, cap), F32)),
        scratch_shapes=[pltpu.VMEM((ne, rr, LANES), F32)],
        compiler_params=pltpu.CompilerParams(vmem_limit_bytes=VMEM_LIMIT),
        name="route",
    )(lg3)


def _ffn_kernel(idxp_ref, idx_ref, idxn_ref, gate_ref, hp_hbm, y_in_hbm, wg_ref, wu_ref, wd_ref, y_hbm,
                gbuf, xbuf, acc, ybuf, sem_h, sem_yi, sem_yo, *, nc):
    del y_in_hbm
    tm = xbuf.shape[0]
    half = D_MODEL // 2
    nwords = half // LANES
    e = pl.program_id(0)
    c = pl.program_id(2)
    nj = pl.num_programs(1)
    step = e * nj + pl.program_id(1)
    last = pl.num_programs(0) * nj - 1
    slot = lax.rem(step, 2)

    def h_copy(ids, r, sl):
        return pltpu.make_async_copy(hp_hbm.at[ids[0, 0, r]], gbuf.at[sl, pl.ds(r * nwords, nwords), :], sem_h.at[sl])

    def yi_copy(ids, r):
        return pltpu.make_async_copy(y_hbm.at[pl.ds(ids[0, 0, r], 1), :], ybuf.at[pl.ds(r, 1), :], sem_yi)

    def yo_copy(ids, r):
        return pltpu.make_async_copy(ybuf.at[pl.ds(r, 1), :], y_hbm.at[pl.ds(ids[0, 0, r], 1), :], sem_yo)

    def for_rows(fn):
        def body(r, carry):
            fn(r)
            return carry
        lax.fori_loop(0, tm, body, 0, unroll=8)

    def start_rows(fn):
        for r in range(tm):
            fn(r).start()

    @pl.when(c == 0)
    def _():
        @pl.when(step == 0)
        def _():
            for_rows(lambda r: h_copy(idx_ref, r, slot).start())

        for_rows(lambda r: h_copy(idx_ref, r, slot).wait())
        for q in range(nwords):
            w = gbuf[slot, pl.ds(q, tm, stride=nwords), :]
            xbuf[:, q * LANES:(q + 1) * LANES] = pltpu.bitcast(w << 16, F32).astype(BF16)
            xbuf[:, half + q * LANES:half + (q + 1) * LANES] = (
                pltpu.bitcast(w & jnp.uint32(0xFFFF0000), F32).astype(BF16))
        acc[...] = jnp.zeros_like(acc)

    @pl.when(c == 1)
    def _():
        @pl.when(step > 0)
        def _():
            for_rows(lambda r: yo_copy(idxp_ref, r).wait())

        start_rows(lambda r: yi_copy(idx_ref, r))

    rows_per_c = tm // nc
    for k in range(rows_per_c):
        h_copy(idxn_ref, c * rows_per_c + k, 1 - slot).start()
    x = xbuf[...]
    g = jnp.dot(x, wg_ref[0], preferred_element_type=F32)
    u = jnp.dot(x, wu_ref[0], preferred_element_type=F32)
    hid = (g * jax.nn.sigmoid(g)) * u
    acc[...] += jnp.dot(hid.astype(BF16), wd_ref[0], preferred_element_type=F32)

    @pl.when(c == nc - 1)
    def _():
        for_rows(lambda r: yi_copy(idx_ref, r).wait())
        onehot = lax.broadcasted_iota(I32, (1, N_EXPERTS), 1) == e
        gcol = jnp.sum(jnp.where(onehot, gate_ref[...], 0.0), axis=1, keepdims=True)
        ybuf[...] = ybuf[...] + acc[...] * gcol
        start_rows(lambda r: yo_copy(idx_ref, r))

        @pl.when(step == last)
        def _():
            for_rows(lambda r: yo_copy(idx_ref, r).wait())
            for_rows(lambda r: h_copy(idxn_ref, r, 1 - slot).wait())


def _ffn(idx3, gate_t, hp3, x1, wg, wu, wd, tm, tf):
    ne = N_EXPERTS
    cap = gate_t.shape[0]
    nj = cap // tm
    nc = EXPERT_FF // tf
    assert nc >= 3, "the row-traffic schedule uses ff chunks 0, 1 and the last one"
    n = x1.shape[0]
    nsteps = ne * nj
    nwords = D_MODEL // 2 // LANES

    def ids(off):
        return pl.BlockSpec((1, 1, tm), lambda e, j, c: (jnp.clip(e * nj + j + off, 0, nsteps - 1), 0, 0),
                            memory_space=pltpu.SMEM)

    return pl.pallas_call(
        functools.partial(_ffn_kernel, nc=nc),
        out_shape=jax.ShapeDtypeStruct((n, D_MODEL), F32),
        grid=(ne, nj, nc),
        in_specs=[
            ids(-1), ids(0), ids(1),
            pl.BlockSpec((tm, ne), lambda e, j, c: (j, 0)),
            pl.BlockSpec(memory_space=pl.ANY),
            pl.BlockSpec(memory_space=pl.ANY),
            pl.BlockSpec((1, D_MODEL, tf), lambda e, j, c: (e, 0, c)),
            pl.BlockSpec((1, D_MODEL, tf), lambda e, j, c: (e, 0, c)),
            pl.BlockSpec((1, tf, D_MODEL), lambda e, j, c: (e, c, 0)),
        ],
        out_specs=pl.BlockSpec(memory_space=pl.ANY),
        scratch_shapes=[
            pltpu.VMEM((2, tm * nwords, LANES), U32),
            pltpu.VMEM((tm, D_MODEL), BF16),
            pltpu.VMEM((tm, D_MODEL), F32),
            pltpu.VMEM((tm, D_MODEL), F32),
            pltpu.SemaphoreType.DMA((2,)),
            pltpu.SemaphoreType.DMA(()),
            pltpu.SemaphoreType.DMA(()),
        ],
        input_output_aliases={5: 0},
        compiler_params=_cparams(("arbitrary", "arbitrary", "arbitrary")),
        name="ffn",
    )(idx3, idx3, idx3, gate_t, hp3, x1, wg, wu, wd)


def _t5_bucket(rel):
    nb = N_BUCKETS // 2
    max_exact = nb // 2
    ret = np.where(rel > 0, nb, 0)
    n = np.abs(rel)
    nf = np.maximum(n, 1).astype(np.float32)
    large = max_exact + (np.log(nf / max_exact) / np.log(MAX_DISTANCE / max_exact) * (nb - max_exact)).astype(np.int32)
    large = np.minimum(large, nb - 1)
    return (ret + np.where(n < max_exact, n, large)).astype(np.int32)


def _win_bias(t5_table):
    rel = np.arange(-WINDOW, WINDOW + 1)
    vals = t5_table[_t5_bucket(rel)].astype(F32).T * LOG2E
    pad = jnp.full((HA, BLOCK - 1), NEG, F32)
    u = jnp.concatenate([pad, vals, pad], axis=1)
    return jnp.stack([u[:, BLOCK - 1 - il:BLOCK - 1 - il + 3 * BLOCK] for il in range(BLOCK)], axis=1)


def _na_bias(rpb, rows):
    ndr, ndc = 2 * NA_ROWS - 1, 2 * NA_COLS - 1
    pad = jnp.full((HB, ndr, GRID_W - NA_COLS), NEG, F32)
    u = jnp.concatenate([pad, rpb.astype(F32) * LOG2E, pad], axis=2)
    col = jnp.stack([u[:, :, GRID_W - 1 - qc:2 * GRID_W - 1 - qc] for qc in range(GRID_W)], axis=2)
    qc = np.arange(GRID_W)[:, None]
    kc = np.arange(GRID_W)[None, :]
    cs = np.clip(qc - NA_COLS // 2, 0, GRID_W - NA_COLS)
    col = jnp.where((kc >= cs) & (kc < cs + NA_COLS), col, NEG)
    masked = jnp.full((HB, GRID_W, GRID_W), NEG, F32)
    nblk = rows // NA_QROWS
    pats = []
    for blk in (0, 1, nblk - 1):
        r0 = blk * NA_QROWS
        ws = int(np.clip(r0 - NA_ROWS // 2, 0, rows - NA_KROWS))
        qrows = []
        for qi in range(NA_QROWS):
            qrow = r0 + qi
            rs = int(np.clip(qrow - NA_ROWS // 2, 0, rows - NA_ROWS))
            blocks = []
            for kj in range(NA_KROWS):
                krow = ws + kj
                blocks.append(col[:, krow - qrow + NA_ROWS - 1] if rs <= krow < rs + NA_ROWS else masked)
            qrows.append(jnp.concatenate(blocks, axis=-1))
        pats.append(jnp.concatenate(qrows, axis=-2))
    return jnp.stack(pats, axis=0)


def _prepare(t5_table, norm_attn, w_in, q_norm_a, k_norm_a, q_norm_b, k_norm_b, sink_a, rpb_b,
             out_norm_a, out_norm_b, w_out, norm_ffn, w_router, w_gate, w_up, w_down):
    qa, ka, va = WIDTH_A, KVA * HEAD_DIM, KVA * HEAD_DIM
    o_ka, o_va, o_qb = qa, qa + ka, qa + ka + va
    o_kb, o_vb = o_qb + WIDTH_B, o_qb + 2 * WIDTH_B
    w_perm = jnp.concatenate([w_in[:, 0:qa], w_in[:, o_qb:o_kb], w_in[:, o_kb:o_vb], w_in[:, o_vb:],
                              w_in[:, o_ka:o_va], w_in[:, o_va:o_qb]], axis=1).astype(BF16)
    scale = HEAD_DIM ** -0.5 * LOG2E
    ones = lambda k: jnp.ones((k,), F32)
    zeros = lambda k: jnp.zeros((k,), F32)
    gain = jnp.concatenate([jnp.tile(q_norm_a, HA) * scale, jnp.tile(q_norm_b, HB) * scale, jnp.tile(k_norm_b, HB),
                            ones(WIDTH_B), jnp.tile(k_norm_a, KVA), ones(va)])[None, :]
    flag = jnp.concatenate([ones(qa), ones(WIDTH_B), ones(WIDTH_B), zeros(WIDTH_B), ones(ka), zeros(va)])[None, :]

    bias_st = _win_bias(t5_table).reshape(KVA, GA * BLOCK, 3 * BLOCK)
    sink_st = jnp.repeat(sink_a.astype(F32) * LOG2E, BLOCK).reshape(KVA, GA * BLOCK, 1)

    wr_hi = w_router.astype(BF16)
    wr_lo = (w_router - wr_hi.astype(F32)).astype(BF16)
    return dict(
        w_perm=w_perm, gain=gain, flag=flag, g_attn=norm_attn[None, :], bias_st=bias_st, sink_st=sink_st,
        rpb=rpb_b, on_a=out_norm_a[None, :], on_b=out_norm_b[None, :],
        wo=w_out.astype(BF16), g_ffn=norm_ffn[None, :], wr2=jnp.concatenate([wr_hi, wr_lo], axis=1),
        wg=w_gate.astype(BF16), wu=w_up.astype(BF16), wd=w_down.astype(BF16))


def _pick(n, pref):
    t = pref
    while n % t:
        t //= 2
    return t


def _trunk(x, p):
    b, s, _ = x.shape
    n = b * s
    x2 = x.reshape(n, D_MODEL)
    proj = _in_proj(x2, p["g_attn"], p["w_perm"], p["gain"], p["flag"], _pick(n, 512), 512)
    proj3 = proj.reshape(b, s, PROJ_WIDTH)
    oa = _win_attn(proj3, p["bias_st"], p["sink_st"], p["on_a"], _pick(s // BLOCK, 4))
    ob = _na_attn(proj3, _na_bias(p["rpb"], s // GRID_W), p["on_b"])
    x1, hp, logits = _out_proj(oa.reshape(n, WIDTH_A), ob.reshape(n, WIDTH_B), p["wo"], x2, p["g_ffn"], p["wr2"],
                               _pick(n, 512))
    cap =CAPACITY_FACTOR * n // N_EXPERTS
    idx, gate = _route(logits.T.reshape(N_EXPERTS, n // LANES, LANES), cap)
    tm = _pick(cap, 1024)
    hp3 = hp.reshape(n, D_MODEL // 2 // LANES, LANES)
    y = _ffn(idx.reshape(N_EXPERTS * (cap // tm), 1, tm), gate.reshape(N_EXPERTS, cap).T, hp3, x1,
             p["wg"], p["wu"], p["wd"], tm, 512)
    return y.reshape(b, s, D_MODEL)


def kernel(x_prompt, x_sample, t5_table, norm_attn, w_in, q_norm_a, k_norm_a, q_norm_b, k_norm_b, sink_a, rpb_b,
           out_norm_a, out_norm_b, w_out, norm_ffn, w_router, w_gate, w_up, w_down):
    p = _prepare(t5_table, norm_attn[0], w_in[0], q_norm_a[0], k_norm_a[0], q_norm_b[0], k_norm_b[0], sink_a[0],
                 rpb_b[0], out_norm_a[0], out_norm_b[0], w_out[0], norm_ffn[0], w_router[0], w_gate[0], w_up[0],
                 w_down[0])
    return (_trunk(x_prompt, p), _trunk(x_sample, p))
```

```python
import functools

import numpy as np
import jax
import jax.numpy as jnp
from jax import lax
from jax.experimental import pallas as pl
from jax.experimental.pallas import tpu as pltpu

F32 = jnp.float32
BF16 = jnp.bfloat16
I32 = jnp.int32
U32 = jnp.uint32

D_MODEL = 2048
HEAD_DIM = 128
HA = 8
KVA = 2
GA = HA // KVA
HB = 8
WIDTH_A = HA * HEAD_DIM
WIDTH_B = HB * HEAD_DIM
PROJ_WIDTH = WIDTH_A + 2 * KVA * HEAD_DIM + 3 * WIDTH_B
WINDOW = 128
BLOCK = 128
N_BUCKETS = 32
MAX_DISTANCE = 128
GRID_W = 64
NA_ROWS = 8
NA_COLS = 16
NA_QROWS = 4
NA_KROWS = 3 * NA_QROWS
N_EXPERTS = 16
EXPERT_FF = D_MODEL
CAPACITY_FACTOR = 2
EPS = 1e-6
NEG = -1e30
LOG2E = 1.4426950408889634
LANES = 128
VMEM_LIMIT = 56 * 1024 * 1024


def _cparams(sem, vmem=VMEM_LIMIT):
    return pltpu.CompilerParams(dimension_semantics=sem, vmem_limit_bytes=vmem)


def _dot_nt(a, b):
    return lax.dot_general(a, b, (((1,), (1,)), ((), ())), preferred_element_type=F32)


def _in_proj_kernel(x_ref, g_ref, w_ref, gain_ref, flag_ref, o_ref, xn_ref, *, tn):
    x = x_ref[...]
    ms = jnp.mean(x * x, axis=-1, keepdims=True)
    xn_ref[...] = (x * lax.rsqrt(ms + EPS) * g_ref[...]).astype(BF16)
    for jt in range(PROJ_WIDTH // tn):
        acc = jnp.dot(xn_ref[...], w_ref[:, jt * tn:(jt + 1) * tn], preferred_element_type=F32)
        for h in range(tn // HEAD_DIM):
            sl = slice(jt * tn + h * HEAD_DIM, jt * tn + (h + 1) * HEAD_DIM)
            a = acc[:, h * HEAD_DIM:(h + 1) * HEAD_DIM]
            r = lax.rsqrt(jnp.mean(a * a, axis=-1, keepdims=True) + EPS)
            scale = jnp.where(flag_ref[:, sl] > 0.0, r, 1.0) * gain_ref[:, sl]
            o_ref[:, sl] = (a * scale).astype(BF16)


def _in_proj(x2, g_attn, w_perm, gain, flag, tm, tn):
    n = x2.shape[0]
    const = lambda i: (0, 0)
    return pl.pallas_call(
        functools.partial(_in_proj_kernel, tn=tn),
        out_shape=jax.ShapeDtypeStruct((n, PROJ_WIDTH), BF16),
        grid=(n // tm,),
        in_specs=[
            pl.BlockSpec((tm, D_MODEL), lambda i: (i, 0)),
            pl.BlockSpec((1, D_MODEL), const),
            pl.BlockSpec((D_MODEL, PROJ_WIDTH), const, pipeline_mode=pl.Buffered(1)),
            pl.BlockSpec((1, PROJ_WIDTH), const),
            pl.BlockSpec((1, PROJ_WIDTH), const),
        ],
        out_specs=pl.BlockSpec((tm, PROJ_WIDTH), lambda i: (i, 0)),
        scratch_shapes=[pltpu.VMEM((tm, D_MODEL), BF16)],
        compiler_params=_cparams(("parallel",)),
        name="in_proj",
    )(x2, g_attn, w_perm, gain, flag)


def _win_kernel(q_ref, kp_ref, kc_ref, kn_ref, vp_ref, vc_ref, vn_ref, bias_ref, sink_ref, g_ref,
                o_ref, obuf_ref, s_ref):
    nsub = q_ref.shape[1] // BLOCK
    i = pl.program_id(1)
    seq = pl.num_programs(1) * nsub * BLOCK
    lane_pos = lax.broadcasted_iota(I32, (1, 3 * BLOCK), 1) - BLOCK

    def window(prev_ref, cur_ref, next_ref, sb, ks):
        lo, hi = (sb - 1) * BLOCK, (sb + 2) * BLOCK
        parts = []
        if lo < 0:
            parts.append(prev_ref[0, :, ks])
        parts.append(cur_ref[0, max(lo, 0):min(hi, nsub * BLOCK), ks])
        if hi > nsub * BLOCK:
            parts.append(next_ref[0, :, ks])
        return jnp.concatenate(parts, axis=0)

    chains = [(sb, kh) for sb in range(nsub) for kh in range(KVA)]
    for ci, (sb, kh) in enumerate(chains):
        rows = slice(sb * BLOCK, (sb + 1) * BLOCK)
        ks = slice(kh * HEAD_DIM, (kh + 1) * HEAD_DIM)
        qs = jnp.concatenate(
            [q_ref[0, rows, (kh * GA + g) * HEAD_DIM:(kh * GA + g + 1) * HEAD_DIM] for g in range(GA)], axis=0)
        s_ref[ci] = _dot_nt(qs, window(kp_ref, kc_ref, kn_ref, sb, ks))
    for ci, (sb, kh) in enumerate(chains):
        rows = slice(sb * BLOCK, (sb + 1) * BLOCK)
        ks = slice(kh * HEAD_DIM, (kh + 1) * HEAD_DIM)
        s = s_ref[ci] + bias_ref[kh]
        if sb == 0 or sb == nsub - 1:
            kpos = (i * nsub + sb) * BLOCK + lane_pos
            s = s + jnp.where((kpos >= 0) & (kpos < seq), 0.0, NEG).astype(F32)
        sk = sink_ref[kh]
        m = jnp.maximum(jnp.max(s, axis=-1, keepdims=True), sk)
        p = jnp.exp2(s - m)
        denom = jnp.sum(p, axis=-1, keepdims=True) + jnp.exp2(sk - m)
        vw = window(vp_ref, vc_ref, vn_ref, sb, ks)
        o = jnp.dot(p.astype(BF16), vw, preferred_element_type=F32) / denom
        for g in range(GA):
            hs = slice((kh * GA + g) * HEAD_DIM, (kh * GA + g + 1) * HEAD_DIM)
            obuf_ref[rows, hs] = o[g * BLOCK:(g + 1) * BLOCK, :]
    oall = obuf_ref[...]
    r = lax.rsqrt(jnp.mean(oall * oall, axis=-1, keepdims=True) + EPS)
    o_ref[0] = (oall * r * g_ref[...]).astype(BF16)


def _win_attn(proj3, bias_st, sink_st, on_a, nsub):
    b, s, _ = proj3.shape
    tq = nsub * BLOCK
    nb = s // BLOCK
    kcol = (WIDTH_A + 3 * WIDTH_B) // (KVA * HEAD_DIM)
    kw = KVA * HEAD_DIM

    def edge(col, side):
        if side < 0:
            return pl.BlockSpec((1, BLOCK, kw), lambda bi, i: (bi, jnp.maximum(i * nsub - 1, 0), col))
        return pl.BlockSpec((1, BLOCK, kw), lambda bi, i: (bi, jnp.minimum(i * nsub + nsub, nb - 1), col))

    def main(col):
        return pl.BlockSpec((1, tq, kw), lambda bi, i: (bi, i, col))

    return pl.pallas_call(
        _win_kernel,
        out_shape=jax.ShapeDtypeStruct((b, s, WIDTH_A), BF16),
        grid=(b, s // tq),
        in_specs=[
            pl.BlockSpec((1, tq, WIDTH_A), lambda bi, i: (bi, i, 0)),
            edge(kcol, -1), main(kcol), edge(kcol, 1),
            edge(kcol + 1, -1), main(kcol + 1), edge(kcol + 1, 1),
            pl.BlockSpec((KVA, GA * BLOCK, 3 * BLOCK), lambda bi, i: (0, 0, 0)),
            pl.BlockSpec((KVA, GA * BLOCK, 1), lambda bi, i: (0, 0, 0)),
            pl.BlockSpec((1, WIDTH_A), lambda bi, i: (0, 0)),
        ],
        out_specs=pl.BlockSpec((1, tq, WIDTH_A), lambda bi, i: (bi, i, 0)),
        scratch_shapes=[pltpu.VMEM((tq, WIDTH_A), F32),
                        pltpu.VMEM((nsub * KVA, GA * BLOCK, 3 * BLOCK), F32)],
        compiler_params=_cparams(("parallel", "arbitrary")),
        name="win_attn",
    )(proj3, proj3, proj3, proj3, proj3, proj3, proj3, bias_st, sink_st, on_a)


def _na_kernel(q_ref, k0_ref, k1_ref, k2_ref, v0_ref, v1_ref, v2_ref, bias_ref, g_ref, o_ref, obuf_ref, s_ref):
    tq = q_ref.shape[1]
    for h in range(HB):
        hs = slice(h * HEAD_DIM, (h + 1) * HEAD_DIM)
        qh = q_ref[0, :, hs]
        for kb, kr in enumerate((k0_ref, k1_ref, k2_ref)):
            s_ref[h, :, kb * tq:(kb + 1) * tq] = _dot_nt(qh, kr[0, :, hs])
    for h in range(HB):
        hs = slice(h * HEAD_DIM, (h + 1) * HEAD_DIM)
        s = s_ref[h] + bias_ref[0, h]
        m = jnp.max(s, axis=-1, keepdims=True)
        p = jnp.exp2(s - m)
        denom = jnp.sum(p, axis=-1, keepdims=True)
        pb = p.astype(BF16)
        pv = jnp.dot(pb[:, 0:tq], v0_ref[0, :, hs], preferred_element_type=F32)
        pv = pv + jnp.dot(pb[:, tq:2 * tq], v1_ref[0, :, hs], preferred_element_type=F32)
        pv = pv + jnp.dot(pb[:, 2 * tq:3 * tq], v2_ref[0, :, hs], preferred_element_type=F32)
        obuf_ref[:, hs] = pv / denom
    oall = obuf_ref[...]
    r = lax.rsqrt(jnp.mean(oall * oall, axis=-1, keepdims=True) + EPS)
    o_ref[0] = (oall * r * g_ref[...]).astype(BF16)


def _na_attn(proj3, bias_pat, on_b):
    b, s, _ = proj3.shape
    tq = NA_QROWS * GRID_W
    nblk = s // tq
    assert nblk >= 4, "row-block patterns (top / interior / bottom) assume at least four row blocks"

    def kvspec(col, off):
        return pl.BlockSpec((1, tq, WIDTH_B), lambda bi, i: (bi, jnp.clip(i - 1, 0, nblk - 3) + off, col))

    def pat(bi, i):
        return (jnp.where(i == 0, 0, jnp.where(i == nblk - 1, 2, 1)), 0, 0, 0)

    return pl.pallas_call(
        _na_kernel,
        out_shape=jax.ShapeDtypeStruct((b, s, WIDTH_B), BF16),
        grid=(b, nblk),
        in_specs=[
            pl.BlockSpec((1, tq, WIDTH_B), lambda bi, i: (bi, i, 1)),
            kvspec(2, 0), kvspec(2, 1), kvspec(2, 2),
            kvspec(3, 0), kvspec(3, 1), kvspec(3, 2),
            pl.BlockSpec((1, HB, tq, 3 * tq), pat),
            pl.BlockSpec((1, WIDTH_B), lambda bi, i: (0, 0)),
        ],
        out_specs=pl.BlockSpec((1, tq, WIDTH_B), lambda bi, i: (bi, i, 0)),
        scratch_shapes=[pltpu.VMEM((tq, WIDTH_B), F32), pltpu.VMEM((HB, tq, 3 * tq), F32)],
        compiler_params=_cparams(("parallel", "arbitrary")),
        name="na_attn",
    )(proj3, proj3, proj3, proj3, proj3, proj3, proj3, bias_pat, on_b)


def _out_proj_kernel(oa_ref, ob_ref, wo_ref, x_ref, g_ref, wr_ref, x1_ref, hp_ref, lg_ref):
    tm = x_ref.shape[0]
    half = D_MODEL // 2
    nwords = half // LANES
    tn = 4 * LANES
    ne = N_EXPERTS
    ssq = jnp.zeros((tm, 1), F32)
    for jt in range(D_MODEL // tn):
        cs = slice(jt * tn, (jt + 1) * tn)
        acc = jnp.dot(oa_ref[...], wo_ref[0:WIDTH_A, cs], preferred_element_type=F32)
        acc = acc + jnp.dot(ob_ref[...], wo_ref[WIDTH_A:WIDTH_A + WIDTH_B, cs], preferred_element_type=F32)
        x1t = x_ref[:, cs] + acc
        x1_ref[:, cs] = x1t
        ssq = ssq + jnp.sum(x1t * x1t, axis=-1, keepdims=True)
    h32 = x1_ref[...] * lax.rsqrt(ssq * (1.0 / D_MODEL) + EPS) * g_ref[...]
    h_hi = h32.astype(BF16)
    h_hi32 = h_hi.astype(F32)
    bits = pltpu.bitcast(h_hi32, U32)
    words = (bits[:, 0:half] >> 16) | (bits[:, half:D_MODEL] & jnp.uint32(0xFFFF0000))
    for q in range(nwords):
        hp_ref[pl.ds(q, tm, stride=nwords), :] = words[:, q * LANES:(q + 1) * LANES]
    h_lo = (h32 - h_hi32).astype(BF16)
    r = jnp.dot(jnp.concatenate([h_hi, h_lo], axis=0), wr_ref[...], preferred_element_type=F32)
    lg_ref[...] = (r[0:tm, 0:ne] + r[0:tm, ne:2 * ne]) + (r[tm:2 * tm, 0:ne] + r[tm:2 * tm, ne:2 * ne])


def _out_proj(oa, ob, wo, x2, g_ffn, wr2, tm):
    n = x2.shape[0]
    nwords = D_MODEL // 2 // LANES
    return pl.pallas_call(
        _out_proj_kernel,
        out_shape=(jax.ShapeDtypeStruct((n, D_MODEL), F32),
                   jax.ShapeDtypeStruct((n * nwords, LANES), U32),
                   jax.ShapeDtypeStruct((n, N_EXPERTS), F32)),
        grid=(n // tm,),
        in_specs=[
            pl.BlockSpec((tm, WIDTH_A), lambda i: (i, 0)),
            pl.BlockSpec((tm, WIDTH_B), lambda i: (i, 0)),
            pl.BlockSpec((WIDTH_A + WIDTH_B, D_MODEL), lambda i: (0, 0), pipeline_mode=pl.Buffered(1)),
            pl.BlockSpec((tm, D_MODEL), lambda i: (i, 0)),
            pl.BlockSpec((1, D_MODEL), lambda i: (0, 0)),
            pl.BlockSpec((D_MODEL, 2 * N_EXPERTS), lambda i: (0, 0)),
        ],
        out_specs=(pl.BlockSpec((tm, D_MODEL), lambda i: (i, 0)),
                   pl.BlockSpec((tm * nwords, LANES), lambda i: (i, 0)),
                   pl.BlockSpec((tm, N_EXPERTS), lambda i: (i, 0))),
        compiler_params=_cparams(("parallel",)),
        name="out_proj",
    )(oa, ob, wo, x2, g_ffn, wr2)


def _route_kernel(lg_ref, idx_ref, gate_ref, aff_ref, *, cap):
    ne, rr, _ = lg_ref.shape
    lg = lg_ref[...]
    ex = jnp.exp(lg - jnp.max(lg, axis=0, keepdims=True))
    aff_ref[...] = ex / jnp.sum(ex, axis=0, keepdims=True)

    upper = (lax.broadcasted_iota(I32, (LANES, LANES), 0) <= lax.broadcasted_iota(I32, (LANES, LANES), 1)).astype(BF16)
    lower = (lax.broadcasted_iota(I32, (rr, rr), 1) < lax.broadcasted_iota(I32, (rr, rr), 0)).astype(BF16)
    pcol = lax.broadcasted_iota(I32, (1, cap), 1).astype(F32)
    rowid = lax.broadcasted_iota(I32, (rr, 1), 0).astype(F32)
    laneid = lax.broadcasted_iota(I32, (LANES, 1), 0).astype(F32)

    def total(v):
        return jnp.sum(jnp.sum(v, axis=1, keepdims=True), axis=0, keepdims=True)

    def prefix(mask):
        within = jnp.dot(mask.astype(BF16), upper, preferred_element_type=F32)
        tot = within[:, LANES - 1:LANES]
        off = jnp.dot(lower, jnp.broadcast_to(tot, (rr, LANES)).astype(BF16), preferred_element_type=F32)
        return within + off, off[:, 0:1], tot

    def gather_rows(vals_t, onehot_t):
        return jnp.dot(vals_t.astype(BF16), onehot_t, preferred_element_type=F32)

    def body(e, carry):
        a = aff_ref[e]
        bits = pltpu.bitcast(a, I32)
        thr = jnp.zeros((1, 1), I32)
        for bit in range(30, -1, -1):
            cand = thr | jnp.int32(1 << bit)
            cnt = total((bits >= cand).astype(F32))
            thr = jnp.where(cnt >= cap, cand, thr)
        gt = bits > thr
        eq = bits == thr
        need = cap - total(gt.astype(F32))
        eq_rank, _, _ = prefix(eq.astype(F32))
        sel = gt | (eq & (eq_rank <= need))
        pos, excl, tot = prefix(sel.astype(F32))
        onehot_t = ((excl <= pcol) & (pcol < excl + tot))
        oh = onehot_t.astype(BF16)
        row_of = jnp.sum(jnp.where(onehot_t, rowid, 0.0), axis=0, keepdims=True)
        pos_t = pos.T
        pos_hi = jnp.floor(pos_t * (1.0 / 64.0))
        pos_lo = pos_t - 64.0 * pos_hi
        prow = 64.0 * gather_rows(pos_hi, oh) + gather_rows(pos_lo, oh)
        local = jnp.sum((prow <= pcol).astype(F32), axis=0, keepdims=True)
        idx_ref[e] = (row_of * float(LANES) + local).astype(I32)
        a_t = a.T
        a1 = a_t.astype(BF16).astype(F32)
        a2 = (a_t - a1).astype(BF16).astype(F32)
        a3 = a_t - a1 - a2
        arow = gather_rows(a1, oh) + gather_rows(a2, oh) + gather_rows(a3, oh)
        gate_ref[e] = jnp.sum(jnp.where(laneid == local, arow, 0.0), axis=0, keepdims=True)
        return carry

    lax.fori_loop(0, ne, body, 0)


def _route(lg3, cap):
    ne, rr, _ = lg3.shape
    return pl.pallas_call(
        functools.partial(_route_kernel, cap=cap),
        out_shape=(jax.ShapeDtypeStruct((ne, 1, cap), I32), jax.ShapeDtypeStruct((ne, 1, cap), F32)),
        scratch_shapes=[pltpu.VMEM((ne, rr, LANES), F32)],
        compiler_params=pltpu.CompilerParams(vmem_limit_bytes=VMEM_LIMIT),
        name="route",
    )(lg3)


def _ffn_kernel(idxp_ref, idx_ref, idxn_ref, gate_ref, hp_hbm, y_in_hbm, wg_ref, wu_ref, wd_ref, y_hbm,
                gbuf, xbuf, acc, ybuf, sem_h, sem_yi, sem_yo, *, nc):
    del y_in_hbm
    tm = xbuf.shape[0]
    half = D_MODEL // 2
    nwords = half // LANES
    e = pl.program_id(0)
    c = pl.program_id(2)
    nj = pl.num_programs(1)
    step = e * nj + pl.program_id(1)
    last = pl.num_programs(0) * nj - 1
    slot = lax.rem(step, 2)

    def h_copy(ids, r, sl):
        return pltpu.make_async_copy(hp_hbm.at[ids[0, 0, r]], gbuf.at[sl, pl.ds(r * nwords, nwords), :], sem_h.at[sl])

    def yi_copy(ids, r):
        return pltpu.make_async_copy(y_hbm.at[pl.ds(ids[0, 0, r], 1), :], ybuf.at[pl.ds(r, 1), :], sem_yi)

    def yo_copy(ids, r):
        return pltpu.make_async_copy(ybuf.at[pl.ds(r, 1), :], y_hbm.at[pl.ds(ids[0, 0, r], 1), :], sem_yo)

    def for_rows(fn):
        def body(r, carry):
            fn(r)
            return carry
        lax.fori_loop(0, tm, body, 0, unroll=8)

    def start_rows(fn):
        for r in range(tm):
            fn(r).start()

    @pl.when(c == 0)
    def _():
        @pl.when(step == 0)
        def _():
            for_rows(lambda r: h_copy(idx_ref, r, slot).start())

        for_rows(lambda r: h_copy(idx_ref, r, slot).wait())
        for q in range(nwords):
            w = gbuf[slot, pl.ds(q, tm, stride=nwords), :]
            xbuf[:, q * LANES:(q + 1) * LANES] = pltpu.bitcast(w << 16, F32).astype(BF16)
            xbuf[:, half + q * LANES:half + (q + 1) * LANES] = (
                pltpu.bitcast(w & jnp.uint32(0xFFFF0000), F32).astype(BF16))
        acc[...] = jnp.zeros_like(acc)

    @pl.when(c == 1)
    def _():
        @pl.when(step > 0)
        def _():
            for_rows(lambda r: yo_copy(idxp_ref, r).wait())

        start_rows(lambda r: yi_copy(idx_ref, r))

    rows_per_c = tm // nc
    for k in range(rows_per_c):
        h_copy(idxn_ref, c * rows_per_c + k, 1 - slot).start()
    x = xbuf[...]
    g = jnp.dot(x, wg_ref[0].astype(BF16), preferred_element_type=F32)
    u = jnp.dot(x, wu_ref[0].astype(BF16), preferred_element_type=F32)
    hid = (g * jax.nn.sigmoid(g)) * u
    acc[...] += jnp.dot(hid.astype(BF16), wd_ref[0].astype(BF16), preferred_element_type=F32)

    @pl.when(c == nc - 1)
    def _():
        for_rows(lambda r: yi_copy(idx_ref, r).wait())
        onehot = lax.broadcasted_iota(I32, (1, N_EXPERTS), 1) == e
        gcol = jnp.sum(jnp.where(onehot, gate_ref[...], 0.0), axis=1, keepdims=True)
        ybuf[...] = ybuf[...] + acc[...] * gcol
        start_rows(lambda r: yo_copy(idx_ref, r))

        @pl.when(step == last)
        def _():
            for_rows(lambda r: yo_copy(idx_ref, r).wait())
            for_rows(lambda r: h_copy(idxn_ref, r, 1 - slot).wait())


def _ffn(idx3, gate_t, hp3, x1, wg, wu, wd, tm, tf):
    ne = N_EXPERTS
    cap = gate_t.shape[0]
    nj = cap // tm
    nc = EXPERT_FF // tf
    assert nc >= 3, "the row-traffic schedule uses ff chunks 0, 1 and the last one"
    n = x1.shape[0]
    nsteps = ne * nj
    nwords = D_MODEL // 2 // LANES

    def ids(off):
        return pl.BlockSpec((1, 1, tm), lambda e, j, c: (jnp.clip(e * nj + j + off, 0, nsteps - 1), 0, 0),
                            memory_space=pltpu.SMEM)

    return pl.pallas_call(
        functools.partial(_ffn_kernel, nc=nc),
        out_shape=jax.ShapeDtypeStruct((n, D_MODEL), F32),
        grid=(ne, nj, nc),
        in_specs=[
            ids(-1), ids(0), ids(1),
            pl.BlockSpec((tm, ne), lambda e, j, c: (j, 0)),
            pl.BlockSpec(memory_space=pl.ANY),
            pl.BlockSpec(memory_space=pl.ANY),
            pl.BlockSpec((1, D_MODEL, tf), lambda e, j, c: (e, 0, c)),
            pl.BlockSpec((1, D_MODEL, tf), lambda e, j, c: (e, 0, c)),
            pl.BlockSpec((1, tf, D_MODEL), lambda e, j, c: (e, c, 0)),
        ],
        out_specs=pl.BlockSpec(memory_space=pl.ANY),
        scratch_shapes=[
            pltpu.VMEM((2, tm * nwords, LANES), U32),
            pltpu.VMEM((tm, D_MODEL), BF16),
            pltpu.VMEM((tm, D_MODEL), F32),
            pltpu.VMEM((tm, D_MODEL), F32),
            pltpu.SemaphoreType.DMA((2,)),
            pltpu.SemaphoreType.DMA(()),
            pltpu.SemaphoreType.DMA(()),
        ],
        input_output_aliases={5: 0},
        compiler_params=_cparams(("arbitrary", "arbitrary", "arbitrary")),
        name="ffn",
    )(idx3, idx3, idx3, gate_t, hp3, x1, wg, wu, wd)


def _t5_bucket(rel):
    nb = N_BUCKETS // 2
    max_exact = nb // 2
    ret = np.where(rel > 0, nb, 0)
    n = np.abs(rel)
    nf = np.maximum(n, 1).astype(np.float32)
    large = max_exact + (np.log(nf / max_exact) / np.log(MAX_DISTANCE / max_exact) * (nb - max_exact)).astype(np.int32)
    large = np.minimum(large, nb - 1)
    return (ret + np.where(n < max_exact, n, large)).astype(np.int32)


def _win_bias(t5_table):
    rel = np.arange(-WINDOW, WINDOW + 1)
    vals = t5_table[_t5_bucket(rel)].astype(F32).T * LOG2E
    pad = jnp.full((HA, BLOCK - 1), NEG, F32)
    u = jnp.concatenate([pad, vals, pad], axis=1)
    return jnp.stack([u[:, BLOCK - 1 - il:BLOCK - 1 - il + 3 * BLOCK] for il in range(BLOCK)], axis=1)


def _na_bias(rpb, rows):
    ndr, ndc = 2 * NA_ROWS - 1, 2 * NA_COLS - 1
    pad = jnp.full((HB, ndr, GRID_W - NA_COLS), NEG, F32)
    u = jnp.concatenate([pad, rpb.astype(F32) * LOG2E, pad], axis=2)
    col = jnp.stack([u[:, :, GRID_W - 1 - qc:2 * GRID_W - 1 - qc] for qc in range(GRID_W)], axis=2)
    qc = np.arange(GRID_W)[:, None]
    kc = np.arange(GRID_W)[None, :]
    cs = np.clip(qc - NA_COLS // 2, 0, GRID_W - NA_COLS)
    col = jnp.where((kc >= cs) & (kc < cs + NA_COLS), col, NEG)
    masked = jnp.full((HB, GRID_W, GRID_W), NEG, F32)
    nblk = rows // NA_QROWS
    pats = []
    for blk in (0, 1, nblk - 1):
        r0 = blk * NA_QROWS
        ws = int(np.clip(r0 - NA_ROWS // 2, 0, rows - NA_KROWS))
        qrows = []
        for qi in range(NA_QROWS):
            qrow = r0 + qi
            rs = int(np.clip(qrow - NA_ROWS // 2, 0, rows - NA_ROWS))
            blocks = []
            for kj in range(NA_KROWS):
                krow = ws + kj
                blocks.append(col[:, krow - qrow + NA_ROWS - 1] if rs <= krow < rs + NA_ROWS else masked)
            qrows.append(jnp.concatenate(blocks, axis=-1))
        pats.append(jnp.concatenate(qrows, axis=-2))
    return jnp.stack(pats, axis=0)


def _prepare(t5_table, norm_attn, w_in, q_norm_a, k_norm_a, q_norm_b, k_norm_b, sink_a, rpb_b,
             out_norm_a, out_norm_b, w_out, norm_ffn, w_router, w_gate, w_up, w_down):
    qa, ka, va = WIDTH_A, KVA * HEAD_DIM, KVA * HEAD_DIM
    o_ka, o_va, o_qb = qa, qa + ka, qa + ka + va
    o_kb, o_vb = o_qb + WIDTH_B, o_qb + 2 * WIDTH_B
    w_perm = jnp.concatenate([w_in[:, 0:qa], w_in[:, o_qb:o_kb], w_in[:, o_kb:o_vb], w_in[:, o_vb:],
                              w_in[:, o_ka:o_va], w_in[:, o_va:o_qb]], axis=1).astype(BF16)
    scale = HEAD_DIM ** -0.5 * LOG2E
    ones = lambda k: jnp.ones((k,), F32)
    zeros = lambda k: jnp.zeros((k,), F32)
    gain = jnp.concatenate([jnp.tile(q_norm_a, HA) * scale, jnp.tile(q_norm_b, HB) * scale, jnp.tile(k_norm_b, HB),
                            ones(WIDTH_B), jnp.tile(k_norm_a, KVA), ones(va)])[None, :]
    flag = jnp.concatenate([ones(qa), ones(WIDTH_B), ones(WIDTH_B), zeros(WIDTH_B), ones(ka), zeros(va)])[None, :]

    bias_st = _win_bias(t5_table).reshape(KVA, GA * BLOCK, 3 * BLOCK)
    sink_st = jnp.repeat(sink_a.astype(F32) * LOG2E, BLOCK).reshape(KVA, GA * BLOCK, 1)

    wr_hi = w_router.astype(BF16)
    wr_lo = (w_router - wr_hi.astype(F32)).astype(BF16)
    return dict(
        w_perm=w_perm, gain=gain, flag=flag, g_attn=norm_attn[None, :], bias_st=bias_st, sink_st=sink_st,
        rpb=rpb_b, on_a=out_norm_a[None, :], on_b=out_norm_b[None, :],
        wo=w_out.astype(BF16), g_ffn=norm_ffn[None, :], wr2=jnp.concatenate([wr_hi, wr_lo], axis=1),
        wg=w_gate, wu=w_up, wd=w_down)


def _pick(n, pref):
    t = pref
    while n % t:
        t //= 2
    return t


def _trunk(x, p):
    b, s, _ = x.shape
    n = b * s
    x2 = x.reshape(n, D_MODEL)
    proj = _in_proj(x2, p["g_attn"], p["w_perm"], p["gain"], p["flag"], _pick(n, 512), 512)
    proj3 = proj.reshape(b, s, PROJ_WIDTH)
    oa = _win_attn(proj3, p["bias_st"], p["sink_st"], p["on_a"], _pick(s // BLOCK, 4))
    ob = _na_attn(proj3, _na_bias(p["rpb"], s // GRID_W), p["on_b"])
    x1, hp, logits = _out_proj(oa.reshape(n, WIDTH_A), ob.reshape(n, WIDTH_B), p["wo"], x2, p["g_ffn"], p["wr2"],
                               _pick(n, 512))
    cap =CAPACITY_FACTOR * n // N_EXPERTS
    idx, gate = _route(logits.T.reshape(N_EXPERTS, n // LANES, LANES), cap)
    tm = _pick(cap, 1024)
    hp3 = hp.reshape(n, D_MODEL // 2 // LANES, LANES)
    y = _ffn(idx.reshape(N_EXPERTS * (cap // tm), 1, tm), gate.reshape(N_EXPERTS, cap).T, hp3, x1,
             p["wg"], p["wu"], p["wd"], tm, 256)
    return y.reshape(b, s, D_MODEL)


def kernel(x_prompt, x_sample, t5_table, norm_attn, w_in, q_norm_a, k_norm_a, q_norm_b, k_norm_b, sink_a, rpb_b,
           out_norm_a, out_norm_b, w_out, norm_ffn, w_router, w_gate, w_up, w_down):
    p = _prepare(t5_table, norm_attn[0], w_in[0], q_norm_a[0], k_norm_a[0], q_norm_b[0], k_norm_b[0], sink_a[0],
                 rpb_b[0], out_norm_a[0], out_norm_b[0], w_out[0], norm_ffn[0], w_router[0], w_gate[0], w_up[0],
                 w_down[0])
    return (_trunk(x_prompt, p), _trunk(x_sample, p))
```

```python
import functools

import numpy as np
import jax
import jax.numpy as jnp
from jax import lax
from jax.experimental import pallas as pl
from jax.experimental.pallas import tpu as pltpu

F32 = jnp.float32
BF16 = jnp.bfloat16
I32 = jnp.int32
U32 = jnp.uint32

D_MODEL = 2048
HEAD_DIM = 128
HA = 8
KVA = 2
GA = HA // KVA
HB = 8
WIDTH_A = HA * HEAD_DIM
WIDTH_B = HB * HEAD_DIM
PROJ_WIDTH = WIDTH_A + 2 * KVA * HEAD_DIM + 3 * WIDTH_B
WINDOW = 128
BLOCK = 128
N_BUCKETS = 32
MAX_DISTANCE = 128
GRID_W = 64
NA_ROWS = 8
NA_COLS = 16
NA_QROWS = 4
NA_KROWS = 3 * NA_QROWS
N_EXPERTS = 16
EXPERT_FF = D_MODEL
CAPACITY_FACTOR = 2
EPS = 1e-6
NEG = -1e30
LOG2E = 1.4426950408889634
LANES = 128
VMEM_LIMIT = 56 * 1024 * 1024


def _cparams(sem, vmem=VMEM_LIMIT):
    return pltpu.CompilerParams(dimension_semantics=sem, vmem_limit_bytes=vmem)


def _dot_nt(a, b):
    return lax.dot_general(a, b, (((1,), (1,)), ((), ())), preferred_element_type=F32)


def _in_proj_kernel(x_ref, g_ref, w_ref, gain_ref, flag_ref, o_ref, xn_ref, *, tn):
    x = x_ref[...]
    ms = jnp.mean(x * x, axis=-1, keepdims=True)
    xn_ref[...] = (x * lax.rsqrt(ms + EPS) * g_ref[...]).astype(BF16)
    for jt in range(PROJ_WIDTH // tn):
        acc = jnp.dot(xn_ref[...], w_ref[:, jt * tn:(jt + 1) * tn], preferred_element_type=F32)
        for h in range(tn // HEAD_DIM):
            sl = slice(jt * tn + h * HEAD_DIM, jt * tn + (h + 1) * HEAD_DIM)
            a = acc[:, h * HEAD_DIM:(h + 1) * HEAD_DIM]
            r = lax.rsqrt(jnp.mean(a * a, axis=-1, keepdims=True) + EPS)
            scale = jnp.where(flag_ref[:, sl] > 0.0, r, 1.0) * gain_ref[:, sl]
            o_ref[:, sl] = (a * scale).astype(BF16)


def _in_proj(x2, g_attn, w_perm, gain, flag, tm, tn):
    n = x2.shape[0]
    const = lambda i: (0, 0)
    return pl.pallas_call(
        functools.partial(_in_proj_kernel, tn=tn),
        out_shape=jax.ShapeDtypeStruct((n, PROJ_WIDTH), BF16),
        grid=(n // tm,),
        in_specs=[
            pl.BlockSpec((tm, D_MODEL), lambda i: (i, 0)),
            pl.BlockSpec((1, D_MODEL), const),
            pl.BlockSpec((D_MODEL, PROJ_WIDTH), const, pipeline_mode=pl.Buffered(1)),
            pl.BlockSpec((1, PROJ_WIDTH), const),
            pl.BlockSpec((1, PROJ_WIDTH), const),
        ],
        out_specs=pl.BlockSpec((tm, PROJ_WIDTH), lambda i: (i, 0)),
        scratch_shapes=[pltpu.VMEM((tm, D_MODEL), BF16)],
        compiler_params=_cparams(("parallel",)),
        name="in_proj",
    )(x2, g_attn, w_perm, gain, flag)


def _win_kernel(q_ref, kp_ref, kc_ref, kn_ref, vp_ref, vc_ref, vn_ref, bias_ref, sink_ref, g_ref,
                o_ref, obuf_ref, s_ref):
    nsub = q_ref.shape[1] // BLOCK
    i = pl.program_id(1)
    seq = pl.num_programs(1) * nsub * BLOCK
    lane_pos = lax.broadcasted_iota(I32, (1, 3 * BLOCK), 1) - BLOCK

    def window(prev_ref, cur_ref, next_ref, sb, ks):
        lo, hi = (sb - 1) * BLOCK, (sb + 2) * BLOCK
        parts = []
        if lo < 0:
            parts.append(prev_ref[0, :, ks])
        parts.append(cur_ref[0, max(lo, 0):min(hi, nsub * BLOCK), ks])
        if hi > nsub * BLOCK:
            parts.append(next_ref[0, :, ks])
        return jnp.concatenate(parts, axis=0)

    chains = [(sb, kh) for sb in range(nsub) for kh in range(KVA)]
    for ci, (sb, kh) in enumerate(chains):
        rows = slice(sb * BLOCK, (sb + 1) * BLOCK)
        ks = slice(kh * HEAD_DIM, (kh + 1) * HEAD_DIM)
        qs = jnp.concatenate(
            [q_ref[0, rows, (kh * GA + g) * HEAD_DIM:(kh * GA + g + 1) * HEAD_DIM] for g in range(GA)], axis=0)
        s_ref[ci] = _dot_nt(qs, window(kp_ref, kc_ref, kn_ref, sb, ks))
    for ci, (sb, kh) in enumerate(chains):
        rows = slice(sb * BLOCK, (sb + 1) * BLOCK)
        ks = slice(kh * HEAD_DIM, (kh + 1) * HEAD_DIM)
        s = s_ref[ci] + bias_ref[kh]
        if sb == 0 or sb == nsub - 1:
            kpos = (i * nsub + sb) * BLOCK + lane_pos
            s = s + jnp.where((kpos >= 0) & (kpos < seq), 0.0, NEG).astype(F32)
        sk = sink_ref[kh]
        m = jnp.maximum(jnp.max(s, axis=-1, keepdims=True), sk)
        p = jnp.exp2(s - m)
        denom = jnp.sum(p, axis=-1, keepdims=True) + jnp.exp2(sk - m)
        vw = window(vp_ref, vc_ref, vn_ref, sb, ks)
        o = jnp.dot(p.astype(BF16), vw, preferred_element_type=F32) / denom
        for g in range(GA):
            hs = slice((kh * GA + g) * HEAD_DIM, (kh * GA + g + 1) * HEAD_DIM)
            obuf_ref[rows, hs] = o[g * BLOCK:(g + 1) * BLOCK, :]
    oall = obuf_ref[...]
    r = lax.rsqrt(jnp.mean(oall * oall, axis=-1, keepdims=True) + EPS)
    o_ref[0] = (oall * r * g_ref[...]).astype(BF16)


def _win_attn(proj3, bias_st, sink_st, on_a, nsub):
    b, s, _ = proj3.shape
    tq = nsub * BLOCK
    nb = s // BLOCK
    kcol = (WIDTH_A + 3 * WIDTH_B) // (KVA * HEAD_DIM)
    kw = KVA * HEAD_DIM

    def edge(col, side):
        if side < 0:
            return pl.BlockSpec((1, BLOCK, kw), lambda bi, i: (bi, jnp.maximum(i * nsub - 1, 0), col))
        return pl.BlockSpec((1, BLOCK, kw), lambda bi, i: (bi, jnp.minimum(i * nsub + nsub, nb - 1), col))

    def main(col):
        return pl.BlockSpec((1, tq, kw), lambda bi, i: (bi, i, col))

    return pl.pallas_call(
        _win_kernel,
        out_shape=jax.ShapeDtypeStruct((b, s, WIDTH_A), BF16),
        grid=(b, s // tq),
        in_specs=[
            pl.BlockSpec((1, tq, WIDTH_A), lambda bi, i: (bi, i, 0)),
            edge(kcol, -1), main(kcol), edge(kcol, 1),
            edge(kcol + 1, -1), main(kcol + 1), edge(kcol + 1, 1),
            pl.BlockSpec((KVA, GA * BLOCK, 3 * BLOCK), lambda bi, i: (0, 0, 0)),
            pl.BlockSpec((KVA, GA * BLOCK, 1), lambda bi, i: (0, 0, 0)),
            pl.BlockSpec((1, WIDTH_A), lambda bi, i: (0, 0)),
        ],
        out_specs=pl.BlockSpec((1, tq, WIDTH_A), lambda bi, i: (bi, i, 0)),
        scratch_shapes=[pltpu.VMEM((tq, WIDTH_A), F32),
                        pltpu.VMEM((nsub * KVA, GA * BLOCK, 3 * BLOCK), F32)],
        compiler_params=_cparams(("parallel", "arbitrary")),
        name="win_attn",
    )(proj3, proj3, proj3, proj3, proj3, proj3, proj3, bias_st, sink_st, on_a)


def _na_kernel(q_ref, k0_ref, k1_ref, k2_ref, v0_ref, v1_ref, v2_ref, bias_ref, g_ref, o_ref, obuf_ref, s_ref):
    tq = q_ref.shape[1]
    for h in range(HB):
        hs = slice(h * HEAD_DIM, (h + 1) * HEAD_DIM)
        qh = q_ref[0, :, hs]
        for kb, kr in enumerate((k0_ref, k1_ref, k2_ref)):
            s_ref[h, :, kb * tq:(kb + 1) * tq] = _dot_nt(qh, kr[0, :, hs])
    for h in range(HB):
        hs = slice(h * HEAD_DIM, (h + 1) * HEAD_DIM)
        s = s_ref[h] + bias_ref[0, h]
        m = jnp.max(s, axis=-1, keepdims=True)
        p = jnp.exp2(s - m)
        denom = jnp.sum(p, axis=-1, keepdims=True)
        pb = p.astype(BF16)
        pv = jnp.dot(pb[:, 0:tq], v0_ref[0, :, hs], preferred_element_type=F32)
        pv = pv + jnp.dot(pb[:, tq:2 * tq], v1_ref[0, :, hs], preferred_element_type=F32)
        pv = pv + jnp.dot(pb[:, 2 * tq:3 * tq], v2_ref[0, :, hs], preferred_element_type=F32)
        obuf_ref[:, hs] = pv / denom
    oall = obuf_ref[...]
    r = lax.rsqrt(jnp.mean(oall * oall, axis=-1, keepdims=True) + EPS)
    o_ref[0] = (oall * r * g_ref[...]).astype(BF16)


def _na_attn(proj3, bias_pat, on_b):
    b, s, _ = proj3.shape
    tq = NA_QROWS * GRID_W
    nblk = s // tq
    assert nblk >= 4, "row-block patterns (top / interior / bottom) assume at least four row blocks"

    def kvspec(col, off):
        return pl.BlockSpec((1, tq, WIDTH_B), lambda bi, i: (bi, jnp.clip(i - 1, 0, nblk - 3) + off, col))

    def pat(bi, i):
        return (jnp.where(i == 0, 0, jnp.where(i == nblk - 1, 2, 1)), 0, 0, 0)

    return pl.pallas_call(
        _na_kernel,
        out_shape=jax.ShapeDtypeStruct((b, s, WIDTH_B), BF16),
        grid=(b, nblk),
        in_specs=[
            pl.BlockSpec((1, tq, WIDTH_B), lambda bi, i: (bi, i, 1)),
            kvspec(2, 0), kvspec(2, 1), kvspec(2, 2),
            kvspec(3, 0), kvspec(3, 1), kvspec(3, 2),
            pl.BlockSpec((1, HB, tq, 3 * tq), pat),
            pl.BlockSpec((1, WIDTH_B), lambda bi, i: (0, 0)),
        ],
        out_specs=pl.BlockSpec((1, tq, WIDTH_B), lambda bi, i: (bi, i, 0)),
        scratch_shapes=[pltpu.VMEM((tq, WIDTH_B), F32), pltpu.VMEM((HB, tq, 3 * tq), F32)],
        compiler_params=_cparams(("parallel", "arbitrary")),
        name="na_attn",
    )(proj3, proj3, proj3, proj3, proj3, proj3, proj3, bias_pat, on_b)


def _out_proj_kernel(oa_ref, ob_ref, wo_ref, x_ref, g_ref, wr_ref, x1_ref, hp_ref, lg_ref):
    tm = x_ref.shape[0]
    half = D_MODEL // 2
    nwords = half // LANES
    tn = 4 * LANES
    ne = N_EXPERTS
    ssq = jnp.zeros((tm, 1), F32)
    for jt in range(D_MODEL // tn):
        cs = slice(jt * tn, (jt + 1) * tn)
        acc = jnp.dot(oa_ref[...], wo_ref[0:WIDTH_A, cs], preferred_element_type=F32)
        acc = acc + jnp.dot(ob_ref[...], wo_ref[WIDTH_A:WIDTH_A + WIDTH_B, cs], preferred_element_type=F32)
        x1t = x_ref[:, cs] + acc
        x1_ref[:, cs] = x1t
        ssq = ssq + jnp.sum(x1t * x1t, axis=-1, keepdims=True)
    h32 = x1_ref[...] * lax.rsqrt(ssq * (1.0 / D_MODEL) + EPS) * g_ref[...]
    h_hi = h32.astype(BF16)
    h_hi32 = h_hi.astype(F32)
    bits = pltpu.bitcast(h_hi32, U32)
    words = (bits[:, 0:half] >> 16) | (bits[:, half:D_MODEL] & jnp.uint32(0xFFFF0000))
    for q in range(nwords):
        hp_ref[pl.ds(q, tm, stride=nwords), :] = words[:, q * LANES:(q + 1) * LANES]
    h_lo = (h32 - h_hi32).astype(BF16)
    r = jnp.dot(jnp.concatenate([h_hi, h_lo], axis=0), wr_ref[...], preferred_element_type=F32)
    lg_ref[...] = (r[0:tm, 0:ne] + r[0:tm, ne:2 * ne]) + (r[tm:2 * tm, 0:ne] + r[tm:2 * tm, ne:2 * ne])


def _out_proj(oa, ob, wo, x2, g_ffn, wr2, tm):
    n = x2.shape[0]
    nwords = D_MODEL // 2 // LANES
    return pl.pallas_call(
        _out_proj_kernel,
        out_shape=(jax.ShapeDtypeStruct((n, D_MODEL), F32),
                   jax.ShapeDtypeStruct((n * nwords, LANES), U32),
                   jax.ShapeDtypeStruct((n, N_EXPERTS), F32)),
        grid=(n // tm,),
        in_specs=[
            pl.BlockSpec((tm, WIDTH_A), lambda i: (i, 0)),
            pl.BlockSpec((tm, WIDTH_B), lambda i: (i, 0)),
            pl.BlockSpec((WIDTH_A + WIDTH_B, D_MODEL), lambda i: (0, 0), pipeline_mode=pl.Buffered(1)),
            pl.BlockSpec((tm, D_MODEL), lambda i: (i, 0)),
            pl.BlockSpec((1, D_MODEL), lambda i: (0, 0)),
            pl.BlockSpec((D_MODEL, 2 * N_EXPERTS), lambda i: (0, 0)),
        ],
        out_specs=(pl.BlockSpec((tm, D_MODEL), lambda i: (i, 0)),
                   pl.BlockSpec((tm * nwords, LANES), lambda i: (i, 0)),
                   pl.BlockSpec((tm, N_EXPERTS), lambda i: (i, 0))),
        compiler_params=_cparams(("parallel",)),
        name="out_proj",
    )(oa, ob, wo, x2, g_ffn, wr2)


def _route_kernel(lg_ref, idx_ref, gate_ref, aff_ref, *, cap):
    ne, rr, _ = lg_ref.shape
    lg = lg_ref[...]
    ex = jnp.exp(lg - jnp.max(lg, axis=0, keepdims=True))
    aff_ref[...] = ex / jnp.sum(ex, axis=0, keepdims=True)

    upper = (lax.broadcasted_iota(I32, (LANES, LANES), 0) <= lax.broadcasted_iota(I32, (LANES, LANES), 1)).astype(BF16)
    lower = (lax.broadcasted_iota(I32, (rr, rr), 1) < lax.broadcasted_iota(I32, (rr, rr), 0)).astype(BF16)
    pcol = lax.broadcasted_iota(I32, (1, cap), 1).astype(F32)
    rowid = lax.broadcasted_iota(I32, (rr, 1), 0).astype(F32)
    laneid = lax.broadcasted_iota(I32, (LANES, 1), 0).astype(F32)

    def total(v):
        return jnp.sum(jnp.sum(v, axis=1, keepdims=True), axis=0, keepdims=True)

    def prefix(mask):
        within = jnp.dot(mask.astype(BF16), upper, preferred_element_type=F32)
        tot = within[:, LANES - 1:LANES]
        off = jnp.dot(lower, jnp.broadcast_to(tot, (rr, LANES)).astype(BF16), preferred_element_type=F32)
        return within + off, off[:, 0:1], tot

    def gather_rows(vals_t, onehot_t):
        return jnp.dot(vals_t.astype(BF16), onehot_t, preferred_element_type=F32)

    def body(e, carry):
        a = aff_ref[e]
        bits = pltpu.bitcast(a, I32)
        thr = jnp.zeros((1, 1), I32)
        for bit in range(30, -1, -1):
            cand = thr | jnp.int32(1 << bit)
            cnt = total((bits >= cand).astype(F32))
            thr = jnp.where(cnt >= cap, cand, thr)
        gt = bits > thr
        eq = bits == thr
        need = cap - total(gt.astype(F32))
        eq_rank, _, _ = prefix(eq.astype(F32))
        sel = gt | (eq & (eq_rank <= need))
        pos, excl, tot = prefix(sel.astype(F32))
        onehot_t = ((excl <= pcol) & (pcol < excl + tot))
        oh = onehot_t.astype(BF16)
        row_of = jnp.sum(jnp.where(onehot_t, rowid, 0.0), axis=0, keepdims=True)
        pos_t = pos.T
        pos_hi = jnp.floor(pos_t * (1.0 / 64.0))
        pos_lo = pos_t - 64.0 * pos_hi
        prow = 64.0 * gather_rows(pos_hi, oh) + gather_rows(pos_lo, oh)
        local = jnp.sum((prow <= pcol).astype(F32), axis=0, keepdims=True)
        idx_ref[e] = (row_of * float(LANES) + local).astype(I32)
        a_t = a.T
        a1 = a_t.astype(BF16).astype(F32)
        a2 = (a_t - a1).astype(BF16).astype(F32)
        a3 = a_t - a1 - a2
        arow = gather_rows(a1, oh) + gather_rows(a2, oh) + gather_rows(a3, oh)
        gate_ref[e] = jnp.sum(jnp.where(laneid == local, arow, 0.0), axis=0, keepdims=True)
        return carry

    lax.fori_loop(0, ne, body, 0)


def _route(lg3, cap):
    ne, rr, _ = lg3.shape
    return pl.pallas_call(
        functools.partial(_route_kernel, cap=cap),
        out_shape=(jax.ShapeDtypeStruct((ne, 1, cap), I32), jax.ShapeDtypeStruct((ne, 1, cap), F32)),
        scratch_shapes=[pltpu.VMEM((ne, rr, LANES), F32)],
        compiler_params=pltpu.CompilerParams(vmem_limit_bytes=VMEM_LIMIT),
        name="route",
    )(lg3)


def _ffn_kernel(idxp_ref, idx_ref, idxn_ref, gate_ref, hp_hbm, y_in_hbm, wg_ref, wu_ref, wd_ref, y_hbm,
                gbuf, xbuf, acc, ybuf, sem_h, sem_yi, sem_yo, *, nc):
    del y_in_hbm
    tm = xbuf.shape[0]
    half = D_MODEL // 2
    nwords = half // LANES
    e = pl.program_id(0)
    c = pl.program_id(2)
    nj = pl.num_programs(1)
    step = e * nj + pl.program_id(1)
    last = pl.num_programs(0) * nj - 1
    slot = lax.rem(step, 2)

    def h_copy(ids, r, sl):
        return pltpu.make_async_copy(hp_hbm.at[ids[0, 0, r]], gbuf.at[sl, pl.ds(r * nwords, nwords), :], sem_h.at[sl])

    def yi_copy(ids, r):
        return pltpu.make_async_copy(y_hbm.at[pl.ds(ids[0, 0, r], 1), :], ybuf.at[pl.ds(r, 1), :], sem_yi)

    def yo_copy(ids, r):
        return pltpu.make_async_copy(ybuf.at[pl.ds(r, 1), :], y_hbm.at[pl.ds(ids[0, 0, r], 1), :], sem_yo)

    def for_rows(fn):
        def body(r, carry):
            fn(r)
            return carry
        lax.fori_loop(0, tm, body, 0, unroll=8)

    def start_rows(fn):
        for r in range(tm):
            fn(r).start(priority=r % 2)

    @pl.when(c == 0)
    def _():
        @pl.when(step == 0)
        def _():
            for_rows(lambda r: h_copy(idx_ref, r, slot).start())

        for_rows(lambda r: h_copy(idx_ref, r, slot).wait())
        for q in range(nwords):
            w = gbuf[slot, pl.ds(q, tm, stride=nwords), :]
            xbuf[:, q * LANES:(q + 1) * LANES] = pltpu.bitcast(w << 16, F32).astype(BF16)
            xbuf[:, half + q * LANES:half + (q + 1) * LANES] = (
                pltpu.bitcast(w & jnp.uint32(0xFFFF0000), F32).astype(BF16))

    @pl.when(c == 1)
    def _():
        @pl.when(step > 0)
        def _():
            for_rows(lambda r: yo_copy(idxp_ref, r).wait())

        start_rows(lambda r: yi_copy(idx_ref, r))

    rows_per_c = tm // nc

    def ff_chunk(first):
        for k in range(rows_per_c):
            h_copy(idxn_ref, c * rows_per_c + k, 1 - slot).start()
        x = xbuf[...]
        g = jnp.dot(x, wg_ref[0].astype(BF16), preferred_element_type=F32)
        u = jnp.dot(x, wu_ref[0].astype(BF16), preferred_element_type=F32)
        hid = (g * jax.nn.sigmoid(g)) * u
        out = jnp.dot(hid.astype(BF16), wd_ref[0].astype(BF16), preferred_element_type=F32)
        if first:
            acc[...] = out
        else:
            acc[...] += out

    @pl.when(c == 0)
    def _():
        ff_chunk(True)

    @pl.when(c > 0)
    def _():
        ff_chunk(False)

    @pl.when(c == nc - 1)
    def _():
        for_rows(lambda r: yi_copy(idx_ref, r).wait())
        onehot = lax.broadcasted_iota(I32, (1, N_EXPERTS), 1) == e
        gcol = jnp.sum(jnp.where(onehot, gate_ref[...], 0.0), axis=1, keepdims=True)
        ybuf[...] = ybuf[...] + acc[...] * gcol
        start_rows(lambda r: yo_copy(idx_ref, r))

        @pl.when(step == last)
        def _():
            for_rows(lambda r: yo_copy(idx_ref, r).wait())
            for_rows(lambda r: h_copy(idxn_ref, r, 1 - slot).wait())


def _ffn(idx3, gate_t, hp3, x1, wg, wu, wd, tm, tf):
    ne = N_EXPERTS
    cap = gate_t.shape[0]
    nj = cap // tm
    nc = EXPERT_FF // tf
    assert nc >= 3, "the row-traffic schedule uses ff chunks 0, 1 and the last one"
    n = x1.shape[0]
    nsteps = ne * nj
    nwords = D_MODEL // 2 // LANES

    def ids(off):
        return pl.BlockSpec((1, 1, tm), lambda e, j, c: (jnp.clip(e * nj + j + off, 0, nsteps - 1), 0, 0),
                            memory_space=pltpu.SMEM)

    return pl.pallas_call(
        functools.partial(_ffn_kernel, nc=nc),
        out_shape=jax.ShapeDtypeStruct((n, D_MODEL), F32),
        grid=(ne, nj, nc),
        in_specs=[
            ids(-1), ids(0), ids(1),
            pl.BlockSpec((tm, ne), lambda e, j, c: (j, 0)),
            pl.BlockSpec(memory_space=pl.ANY),
            pl.BlockSpec(memory_space=pl.ANY),
            pl.BlockSpec((1, D_MODEL, tf), lambda e, j, c: (e, 0, c)),
            pl.BlockSpec((1, D_MODEL, tf), lambda e, j, c: (e, 0, c)),
            pl.BlockSpec((1, tf, D_MODEL), lambda e, j, c: (e, c, 0)),
        ],
        out_specs=pl.BlockSpec(memory_space=pl.ANY),
        scratch_shapes=[
            pltpu.VMEM((2, tm * nwords, LANES), U32),
            pltpu.VMEM((tm, D_MODEL), BF16),
            pltpu.VMEM((tm, D_MODEL), F32),
            pltpu.VMEM((tm, D_MODEL), F32),
            pltpu.SemaphoreType.DMA((2,)),
            pltpu.SemaphoreType.DMA(()),
            pltpu.SemaphoreType.DMA(()),
        ],
        input_output_aliases={5: 0},
        compiler_params=_cparams(("arbitrary", "arbitrary", "arbitrary")),
        name="ffn",
    )(idx3, idx3, idx3, gate_t, hp3, x1, wg, wu, wd)


def _t5_bucket(rel):
    nb = N_BUCKETS // 2
    max_exact = nb // 2
    ret = np.where(rel > 0, nb, 0)
    n = np.abs(rel)
    nf = np.maximum(n, 1).astype(np.float32)
    large = max_exact + (np.log(nf / max_exact) / np.log(MAX_DISTANCE / max_exact) * (nb - max_exact)).astype(np.int32)
    large = np.minimum(large, nb - 1)
    return (ret + np.where(n < max_exact, n, large)).astype(np.int32)


def _win_bias(t5_table):
    rel = np.arange(-WINDOW, WINDOW + 1)
    vals = t5_table[_t5_bucket(rel)].astype(F32).T * LOG2E
    pad = jnp.full((HA, BLOCK - 1), NEG, F32)
    u = jnp.concatenate([pad, vals, pad], axis=1)
    return jnp.stack([u[:, BLOCK - 1 - il:BLOCK - 1 - il + 3 * BLOCK] for il in range(BLOCK)], axis=1)


def _na_bias(rpb, rows):
    ndr, ndc = 2 * NA_ROWS - 1, 2 * NA_COLS - 1
    pad = jnp.full((HB, ndr, GRID_W - NA_COLS), NEG, F32)
    u = jnp.concatenate([pad, rpb.astype(F32) * LOG2E, pad], axis=2)
    col = jnp.stack([u[:, :, GRID_W - 1 - qc:2 * GRID_W - 1 - qc] for qc in range(GRID_W)], axis=2)
    qc = np.arange(GRID_W)[:, None]
    kc = np.arange(GRID_W)[None, :]
    cs = np.clip(qc - NA_COLS // 2, 0, GRID_W - NA_COLS)
    col = jnp.where((kc >= cs) & (kc < cs + NA_COLS), col, NEG)
    masked = jnp.full((HB, GRID_W, GRID_W), NEG, F32)
    nblk = rows // NA_QROWS
    pats = []
    for blk in (0, 1, nblk - 1):
        r0 = blk * NA_QROWS
        ws = int(np.clip(r0 - NA_ROWS // 2, 0, rows - NA_KROWS))
        qrows = []
        for qi in range(NA_QROWS):
            qrow = r0 + qi
            rs = int(np.clip(qrow - NA_ROWS // 2, 0, rows - NA_ROWS))
            blocks = []
            for kj in range(NA_KROWS):
                krow = ws + kj
                blocks.append(col[:, krow - qrow + NA_ROWS - 1] if rs <= krow < rs + NA_ROWS else masked)
            qrows.append(jnp.concatenate(blocks, axis=-1))
        pats.append(jnp.concatenate(qrows, axis=-2))
    return jnp.stack(pats, axis=0)


def _prepare(t5_table, norm_attn, w_in, q_norm_a, k_norm_a, q_norm_b, k_norm_b, sink_a, rpb_b,
             out_norm_a, out_norm_b, w_out, norm_ffn, w_router, w_gate, w_up, w_down):
    qa, ka, va = WIDTH_A, KVA * HEAD_DIM, KVA * HEAD_DIM
    o_ka, o_va, o_qb = qa, qa + ka, qa + ka + va
    o_kb, o_vb = o_qb + WIDTH_B, o_qb + 2 * WIDTH_B
    w_perm = jnp.concatenate([w_in[:, 0:qa], w_in[:, o_qb:o_kb], w_in[:, o_kb:o_vb], w_in[:, o_vb:],
                              w_in[:, o_ka:o_va], w_in[:, o_va:o_qb]], axis=1).astype(BF16)
    scale = HEAD_DIM ** -0.5 * LOG2E
    ones = lambda k: jnp.ones((k,), F32)
    zeros = lambda k: jnp.zeros((k,), F32)
    gain = jnp.concatenate([jnp.tile(q_norm_a, HA) * scale, jnp.tile(q_norm_b, HB) * scale, jnp.tile(k_norm_b, HB),
                            ones(WIDTH_B), jnp.tile(k_norm_a, KVA), ones(va)])[None, :]
    flag = jnp.concatenate([ones(qa), ones(WIDTH_B), ones(WIDTH_B), zeros(WIDTH_B), ones(ka), zeros(va)])[None, :]

    bias_st = _win_bias(t5_table).reshape(KVA, GA * BLOCK, 3 * BLOCK)
    sink_st = jnp.repeat(sink_a.astype(F32) * LOG2E, BLOCK).reshape(KVA, GA * BLOCK, 1)

    wr_hi = w_router.astype(BF16)
    wr_lo = (w_router - wr_hi.astype(F32)).astype(BF16)
    return dict(
        w_perm=w_perm, gain=gain, flag=flag, g_attn=norm_attn[None, :], bias_st=bias_st, sink_st=sink_st,
        rpb=rpb_b, on_a=out_norm_a[None, :], on_b=out_norm_b[None, :],
        wo=w_out.astype(BF16), g_ffn=norm_ffn[None, :], wr2=jnp.concatenate([wr_hi, wr_lo], axis=1),
        wg=w_gate, wu=w_up, wd=w_down)


def _pick(n, pref):
    t = pref
    while n % t:
        t //= 2
    return t


def _trunk(x, p):
    b, s, _ = x.shape
    n = b * s
    x2 = x.reshape(n, D_MODEL)
    proj = _in_proj(x2, p["g_attn"], p["w_perm"], p["gain"], p["flag"], _pick(n, 512), 512)
    proj3 = proj.reshape(b, s, PROJ_WIDTH)
    oa = _win_attn(proj3, p["bias_st"], p["sink_st"], p["on_a"], _pick(s // BLOCK, 4))
    ob = _na_attn(proj3, _na_bias(p["rpb"], s // GRID_W), p["on_b"])
    x1, hp, logits = _out_proj(oa.reshape(n, WIDTH_A), ob.reshape(n, WIDTH_B), p["wo"], x2, p["g_ffn"], p["wr2"],
                               _pick(n, 512))
    cap =CAPACITY_FACTOR * n // N_EXPERTS
    idx, gate = _route(logits.T.reshape(N_EXPERTS, n // LANES, LANES), cap)
    tm = _pick(cap, 1024)
    hp3 = hp.reshape(n, D_MODEL // 2 // LANES, LANES)
    y = _ffn(idx.reshape(N_EXPERTS * (cap // tm), 1, tm), gate.reshape(N_EXPERTS, cap).T, hp3, x1,
             p["wg"], p["wu"], p["wd"], tm, 256)
    return y.reshape(b, s, D_MODEL)


def kernel(x_prompt, x_sample, t5_table, norm_attn, w_in, q_norm_a, k_norm_a, q_norm_b, k_norm_b, sink_a, rpb_b,
           out_norm_a, out_norm_b, w_out, norm_ffn, w_router, w_gate, w_up, w_down):
    p = _prepare(t5_table, norm_attn[0], w_in[0], q_norm_a[0], k_norm_a[0], q_norm_b[0], k_norm_b[0], sink_a[0],
                 rpb_b[0], out_norm_a[0], out_norm_b[0], w_out[0], norm_ffn[0], w_router[0], w_gate[0], w_up[0],
                 w_down[0])
    return (_trunk(x_prompt, p), _trunk(x_sample, p))
```
